```python
import jax, jax.numpy as jnp
from jax import lax
import numpy as np

D_MODEL = 1024
BATCH = 16
SEQ = 4096
DEPTH = 2
DEC_BATCH = 32
DEC_SEQ = 2048
PAST_LEN = 128

N_MIXERS = 2
HEAD_DIM = 64
MIX_WIDTH = D_MODEL
XA_HEADS = 4
XA_WIDTH = XA_HEADS * HEAD_DIM
TOK_WIDTH = MIX_WIDTH - XA_WIDTH
A_GROUPS = TOK_WIDTH // HEAD_DIM
CHUNK = 128
B_HEADS = TOK_WIDTH // HEAD_DIM
GRID_W = 64
WIN_H = 8
WIN_W = 16
N_MEM = 256
D_FF = 2816
N_EXPERTS = 8
TOP_K = 2
D_FF_EXPERT = 3584
MOE_BLOCK = 256
N_A_LAYERS = (DEPTH + 1) // 2
N_B_LAYERS = DEPTH // 2
RMS_EPS = 1e-6
NEG_INF = -1e30

kernel_name = "hybrid_gmlp_natten_memxattn_moe_encoder"


def rms_norm(x, g):
    xf = x.astype(jnp.float32)
    y = xf * lax.rsqrt(jnp.mean(xf * xf, axis=-1, keepdims=True) + RMS_EPS)
    return (y * g.astype(jnp.float32)).astype(x.dtype)


def swiglu(h, w_gu, w_down):
    g, u = jnp.split(h @ w_gu, 2, axis=-1)
    return (jax.nn.silu(g) * u) @ w_down


def chunked_spatial_gating(z, g_v, w_s, b_s):
    u, v = jnp.split(z, 2, axis=-1)
    v = rms_norm(v, g_v)
    b, s, _ = v.shape
    vg = v.reshape(b, s // CHUNK, CHUNK, A_GROUPS, HEAD_DIM)
    sv = jnp.einsum('gpq,bnqgc->bnpgc', w_s, vg) + b_s.T[None, None, :, :, None]
    return u * sv.reshape(b, s, TOK_WIDTH)


def neighbourhood_attention(q, k, v, rpb):
    b, s, h, dh = q.shape
    rows = s // GRID_W
    kh = min(WIN_H, rows)
    q = q.reshape(b, rows, GRID_W, h, dh)
    k = k.reshape(b, rows, GRID_W, h, dh)
    v = v.reshape(b, rows, GRID_W, h, dh)
    r = jnp.arange(rows)
    row_idx = jnp.clip(r - kh // 2, 0, rows - kh)[:, None] + jnp.arange(kh)[None, :]
    kg = k[:, row_idx]
    vg = v[:, row_idx]
    c = jnp.arange(GRID_W)
    col_start = jnp.clip(c - WIN_W // 2, 0, GRID_W - WIN_W)
    col_in = (c[None, :] >= col_start[:, None]) & (c[None, :] < col_start[:, None] + WIN_W)
    rel_row = row_idx - r[:, None] + (WIN_H - 1)
    rel_col = jnp.clip(c[None, :] - c[:, None] + (WIN_W - 1), 0, 2 * WIN_W - 2)
    bias = rpb[:, rel_row[:, None, :, None], rel_col[None, :, None, :]]
    scores = jnp.einsum('brqhd,brikhd->bhrqik', q, kg, preferred_element_type=jnp.float32)
    scores = scores * (HEAD_DIM ** -0.5) + bias.astype(jnp.float32)[None]
    scores = jnp.where(col_in[:, None, :], scores, NEG_INF)
    p = jax.nn.softmax(scores, axis=(-2, -1)).astype(v.dtype)
    o = jnp.einsum('bhrqik,brikhd->brqhd', p, vg)
    return o.reshape(b, s, h * dh)


def memory_cross_attention(q, mem_n, w_kv):
    b, m, _ = mem_n.shape
    k, v = jnp.split(mem_n @ w_kv, 2, axis=-1)
    k = k.reshape(b, m, XA_HEADS, HEAD_DIM)
    v = v.reshape(b, m, XA_HEADS, HEAD_DIM)
    s = jnp.einsum('bshd,bmhd->bhsm', q, k, preferred_element_type=jnp.float32) * (HEAD_DIM ** -0.5)
    p = jax.nn.softmax(s, axis=-1).astype(v.dtype)
    o = jnp.einsum('bhsm,bmhd->bshd', p, v)
    return o.reshape(q.shape[0], q.shape[1], XA_WIDTH)


def moe_swiglu(h, w_router, w_gu, w_down):
    t, d = h.shape
    logits = jnp.matmul(h, w_router, preferred_element_type=jnp.float32)
    top_val, top_idx = lax.top_k(logits, TOP_K)
    gates = jax.nn.softmax(top_val, axis=-1)
    n = t * TOP_K
    flat_e = top_idx.reshape(n)
    order = jnp.argsort(flat_e)
    sorted_e = flat_e[order]
    sorted_tok = order // TOP_K
    sorted_gate = gates.reshape(n)[order].astype(h.dtype)
    counts = jnp.bincount(flat_e, length=N_EXPERTS)
    padded = (counts + MOE_BLOCK - 1) // MOE_BLOCK * MOE_BLOCK
    pad_end = jnp.cumsum(padded)
    pad_start = pad_end - padded
    start = jnp.cumsum(counts) - counts
    dest = pad_start[sorted_e] + jnp.arange(n) - start[sorted_e]
    n_blocks = -(-n // MOE_BLOCK) + N_EXPERTS
    xs = jnp.zeros((n_blocks * MOE_BLOCK, d), h.dtype).at[dest].set(h[sorted_tok])
    block_e = jnp.minimum(jnp.searchsorted(pad_end, jnp.arange(n_blocks) * MOE_BLOCK, side='right'), N_EXPERTS - 1)

    def expert_block(args):
        xb, e = args
        return swiglu(xb, w_gu[e], w_down[e])

    ys = lax.map(expert_block, (xs.reshape(n_blocks, MOE_BLOCK, d), block_e)).reshape(n_blocks * MOE_BLOCK, d)
    return jnp.zeros((t, d), ys.dtype).at[sorted_tok].add(ys[dest] * sorted_gate[:, None])


def encoder_trunk(x, mem, p):
    b, s, d = x.shape
    for l in range(DEPTH):
        h = rms_norm(x, p['norm_mix'][l])
        mem_n = rms_norm(mem, p['norm_mem'][l])
        j = l // 2
        if l % N_MIXERS == 0:
            proj = h @ p['w_in_a'][j]
            z, qx = proj[..., :2 * TOK_WIDTH], proj[..., 2 * TOK_WIDTH:]
            tok = chunked_spatial_gating(jax.nn.gelu(z, approximate=False),
                                         p['g_v_a'][j], p['w_s_a'][j], p['b_s_a'][j])
        else:
            proj = h @ p['w_in_b'][j]
            q = proj[..., :TOK_WIDTH].reshape(b, s, B_HEADS, HEAD_DIM)
            k = proj[..., TOK_WIDTH:2 * TOK_WIDTH].reshape(b, s, B_HEADS, HEAD_DIM)
            v = proj[..., 2 * TOK_WIDTH:3 * TOK_WIDTH].reshape(b, s, B_HEADS, HEAD_DIM)
            qx = proj[..., 3 * TOK_WIDTH:]
            tok = neighbourhood_attention(q, k, v, p['rpb_b'][j])
        xa = memory_cross_attention(qx.reshape(b, s, XA_HEADS, HEAD_DIM), mem_n, p['w_mem_kv'][l])
        x = x + jnp.concatenate([tok, xa], axis=-1) @ p['w_out'][l]
        h = rms_norm(x, p['norm_ffn'][l])
        if l % 2 == 0:
            f = swiglu(h, p['w_ffn_gu'][j], p['w_ffn_down'][j])
        else:
            f = moe_swiglu(h.reshape(b * s, d), p['w_router'][j], p['w_exp_gu'][j],
                           p['w_exp_down'][j]).reshape(b, s, d)
        x = x + f
    return rms_norm(x, p['norm_final'])


def setup_inputs(seed: int = 0) -> dict:
    key = jax.random.key(seed)
    ks = jax.random.split(key, 24)

    def nrm(k, shape, scale):
        return jax.random.normal(k, shape, jnp.float32) * scale

    return {
        'x_prompt': nrm(ks[0], (BATCH, SEQ, D_MODEL), 1.0),
        'x_sample': nrm(ks[1], (DEC_BATCH, DEC_SEQ, D_MODEL), 1.0),
        'mem_prompt': nrm(ks[2], (BATCH, N_MEM, D_MODEL), 1.0),
        'mem_sample': nrm(ks[3], (DEC_BATCH, N_MEM, D_MODEL), 1.0),
        'norm_mix': 1.0 + nrm(ks[4], (DEPTH, D_MODEL), 0.02),
        'norm_mem': 1.0 + nrm(ks[5], (DEPTH, D_MODEL), 0.02),
        'norm_ffn': 1.0 + nrm(ks[6], (DEPTH, D_MODEL), 0.02),
        'norm_final': 1.0 + nrm(ks[7], (D_MODEL,), 0.02),
        'w_in_a': nrm(ks[8], (N_A_LAYERS, D_MODEL, 2 * TOK_WIDTH + XA_WIDTH), D_MODEL ** -0.5),
        'g_v_a': 1.0 + nrm(ks[9], (N_A_LAYERS, TOK_WIDTH), 0.02),
        'w_s_a': nrm(ks[10], (N_A_LAYERS, A_GROUPS, CHUNK, CHUNK), CHUNK ** -0.5),
        'b_s_a': 1.0 + nrm(ks[11], (N_A_LAYERS, A_GROUPS, CHUNK), 0.1),
        'w_in_b': nrm(ks[12], (N_B_LAYERS, D_MODEL, 3 * TOK_WIDTH + XA_WIDTH), D_MODEL ** -0.5),
        'rpb_b': nrm(ks[13], (N_B_LAYERS, B_HEADS, 2 * WIN_H - 1, 2 * WIN_W - 1), 0.5),
        'w_mem_kv': nrm(ks[14], (DEPTH, D_MODEL, 2 * XA_WIDTH), D_MODEL ** -0.5),
        'w_out': nrm(ks[15], (DEPTH, MIX_WIDTH, D_MODEL), MIX_WIDTH ** -0.5),
        'w_ffn_gu': nrm(ks[16], (N_A_LAYERS, D_MODEL, 2 * D_FF), D_MODEL ** -0.5),
        'w_ffn_down': nrm(ks[17], (N_A_LAYERS, D_FF, D_MODEL), D_FF ** -0.5),
        'w_router': nrm(ks[18], (N_B_LAYERS, D_MODEL, N_EXPERTS), D_MODEL ** -0.5),
        'w_exp_gu': nrm(ks[19], (N_B_LAYERS, N_EXPERTS, D_MODEL, 2 * D_FF_EXPERT), D_MODEL ** -0.5),
        'w_exp_down': nrm(ks[20], (N_B_LAYERS, N_EXPERTS, D_FF_EXPERT, D_MODEL), D_FF_EXPERT ** -0.5),
    }


def reference(x_prompt, x_sample, mem_prompt, mem_sample, norm_mix, norm_mem, norm_ffn, norm_final,
              w_in_a, g_v_a, w_s_a, b_s_a, w_in_b, rpb_b, w_mem_kv, w_out,
              w_ffn_gu, w_ffn_down, w_router, w_exp_gu, w_exp_down):
    params = dict(norm_mix=norm_mix, norm_mem=norm_mem, norm_ffn=norm_ffn, norm_final=norm_final,
                  w_in_a=w_in_a, g_v_a=g_v_a, w_s_a=w_s_a, b_s_a=b_s_a, w_in_b=w_in_b, rpb_b=rpb_b,
                  w_mem_kv=w_mem_kv, w_out=w_out, w_ffn_gu=w_ffn_gu, w_ffn_down=w_ffn_down,
                  w_router=w_router, w_exp_gu=w_exp_gu, w_exp_down=w_exp_down)
    y_prompt = encoder_trunk(x_prompt, mem_prompt, params)
    y_sample = encoder_trunk(x_sample, mem_sample, params)
    return (y_prompt, y_sample)
```

```python
import functools

import numpy as np
import jax
import jax.numpy as jnp
from jax import lax
from jax.experimental import pallas as pl
from jax.experimental.pallas import tpu as pltpu

D_MODEL = 1024
HEAD_DIM = 64
XA_HEADS = 4
XA_WIDTH = XA_HEADS * HEAD_DIM
TOK_WIDTH = D_MODEL - XA_WIDTH
CHUNK = 128
N_PAIR = TOK_WIDTH // (2 * HEAD_DIM)
GRID_W = 64
WIN_H = 8
WIN_W = 16
N_MEM = 256
D_FF = 2816
N_EXPERTS = 8
D_FF_EXPERT = 3584
RMS_EPS = 1e-6
NEG_INF = -1e30
ATTN_SCALE = HEAD_DIM ** -0.5

LANES = 128
SUBLANES = 8
VMEM_LIMIT = 56 * 1024 * 1024

NAT_ROWS_PER_STEP = 2
NAT_WIN_ROWS = WIN_H + NAT_ROWS_PER_STEP - 1
NAT_KEYS = NAT_WIN_ROWS * GRID_W
NAT_PATTERNS = 5

MOE_ROWS = 256
FF_CHUNK_DENSE = 1408
FF_CHUNK_EXPERT = 512

_BF16 = jnp.bfloat16
_F32 = jnp.float32
_NT = (((1,), (1,)), ((), ()))
_TN = (((0,), (0,)), ((), ()))


def _params(*semantics):
    return pltpu.CompilerParams(dimension_semantics=semantics, vmem_limit_bytes=VMEM_LIMIT)


def _const_spec(shape):
    zeros = (0,) * len(shape)
    return pl.BlockSpec(shape, lambda *_: zeros, pipeline_mode=pl.Buffered(1))


def _rms(x, g):
    ms = jnp.mean(x * x, axis=-1, keepdims=True)
    return x * lax.rsqrt(ms + RMS_EPS) * g


def _dot(a, b):
    return jnp.dot(a, b, preferred_element_type=_F32)


def _gelu(x):
    return 0.5 * x * (1.0 + lax.erf(x * np.float32(1.0 / np.sqrt(2.0))))


def _silu(x):
    return x / (1.0 + jnp.exp(-x))


def _softmax_rows(s):
    m = jnp.max(s, axis=-1, keepdims=True)
    e = jnp.exp(s - m)
    return e / jnp.sum(e, axis=-1, keepdims=True)


def _xattn_outproj(x, tok, qx, k, v, wout_ref):
    lane = lax.broadcasted_iota(jnp.int32, (1, XA_WIDTH), 1)
    qx = qx.astype(_F32)
    xa = jnp.zeros(qx.shape, _F32)
    for h in range(XA_HEADS):
        in_head = (lane >= h * HEAD_DIM) & (lane < (h + 1) * HEAD_DIM)
        qh = jnp.where(in_head, qx, 0.0).astype(_BF16)
        s = lax.dot_general(qh, k, _NT, preferred_element_type=_F32)
        p = _softmax_rows(s).astype(_BF16)
        xa = jnp.where(in_head, _dot(p, v), xa)
    y = _dot(tok, wout_ref[:TOK_WIDTH, :]) + _dot(xa.astype(_BF16), wout_ref[TOK_WIDTH:, :])
    return x + y


def _mem_kv_kernel(mem_ref, g_ref, w_ref, k_ref, v_ref):
    m = _rms(mem_ref[0], g_ref[...]).astype(_BF16)
    kv = _dot(m, w_ref[...])
    k_ref[0] = kv[:, :XA_WIDTH].astype(_BF16)
    v_ref[0] = kv[:, XA_WIDTH:].astype(_BF16)


def _mem_kv(mem, g, w_kv):
    b = mem.shape[0]
    out = jax.ShapeDtypeStruct((b, N_MEM, XA_WIDTH), _BF16)
    return pl.pallas_call(
        _mem_kv_kernel,
        grid=(b,),
        in_specs=[pl.BlockSpec((1, N_MEM, D_MODEL), lambda i: (i, 0, 0)),
                  _const_spec((1, D_MODEL)),
                  _const_spec((D_MODEL, 2 * XA_WIDTH))],
        out_specs=[pl.BlockSpec((1, N_MEM, XA_WIDTH), lambda i: (i, 0, 0))] * 2,
        out_shape=[out, out],
        compiler_params=_params("parallel"),
        name="mem_kv",
    )(mem, g, w_kv)


def _mixer_a_kernel(x_ref, k_ref, v_ref, g_ref, win_ref, gv_ref, ws_ref, bs_ref, wout_ref, o_ref, *, tm):
    x = x_ref[0]
    h = _rms(x, g_ref[...]).astype(_BF16)
    proj = _dot(h, win_ref[...])
    z = _gelu(proj[:, :2 * TOK_WIDTH])
    u = z[:, :TOK_WIDTH]
    vn = _rms(z[:, TOK_WIDTH:], gv_ref[...]).astype(_BF16)
    low_half = lax.broadcasted_iota(jnp.int32, (1, LANES), 1) < HEAD_DIM
    bias = bs_ref[...]
    toks = []
    for c in range(tm // CHUNK):
        vc = vn[c * CHUNK:(c + 1) * CHUNK]
        cols = []
        for j in range(N_PAIR):
            r = _dot(ws_ref[j], vc[:, j * LANES:(j + 1) * LANES])
            cols.append(jnp.where(low_half, r[:CHUNK], r[CHUNK:]))
        sv = jnp.concatenate(cols, axis=1) + bias
        toks.append(u[c * CHUNK:(c + 1) * CHUNK] * sv)
    tok = jnp.concatenate(toks, axis=0).astype(_BF16)
    qx = proj[:, 2 * TOK_WIDTH:] * ATTN_SCALE
    o_ref[0] = _xattn_outproj(x, tok, qx, k_ref[0], v_ref[0], wout_ref)


def _mixer_a(x, k, v, g, w_in, g_v, w_s2, b_s2, w_out, *, tm):
    b, s, _ = x.shape
    return pl.pallas_call(
        functools.partial(_mixer_a_kernel, tm=tm),
        grid=(b, s // tm),
        in_specs=[pl.BlockSpec((1, tm, D_MODEL), lambda i, j: (i, j, 0)),
                  pl.BlockSpec((1, N_MEM, XA_WIDTH), lambda i, j: (i, 0, 0)),
                  pl.BlockSpec((1, N_MEM, XA_WIDTH), lambda i, j: (i, 0, 0)),
                  _const_spec((1, D_MODEL)),
                  _const_spec((D_MODEL, 2 * TOK_WIDTH + XA_WIDTH)),
                  _const_spec((1, TOK_WIDTH)),
                  _const_spec((N_PAIR, 2 * CHUNK, CHUNK)),
                  _const_spec((CHUNK, TOK_WIDTH)),
                  _const_spec((D_MODEL, D_MODEL))],
        out_specs=pl.BlockSpec((1, tm, D_MODEL), lambda i, j: (i, j, 0)),
        out_shape=jax.ShapeDtypeStruct(x.shape, _F32),
        compiler_params=_params("parallel", "parallel"),
        name="mixer_gmlp",
    )(x, k, v, g, w_in, g_v, w_s2, b_s2, w_out)


def _ffn_kernel(x_ref, g_ref, wgu_ref, wd_ref, o_ref):
    x = x_ref[...]
    h = _rms(x, g_ref[...]).astype(_BF16)
    acc = x
    for c in range(D_FF // FF_CHUNK_DENSE):
        lo = c * FF_CHUNK_DENSE
        gate = _dot(h, wgu_ref[:, lo:lo + FF_CHUNK_DENSE])
        up = _dot(h, wgu_ref[:, D_FF + lo:D_FF + lo + FF_CHUNK_DENSE])
        a = (_silu(gate) * up).astype(_BF16)
        acc = acc + _dot(a, wd_ref[lo:lo + FF_CHUNK_DENSE, :])
    o_ref[...] = acc


def _ffn(x2, g, w_gu, w_down, *, tm):
    t = x2.shape[0]
    return pl.pallas_call(
        _ffn_kernel,
        grid=(t // tm,),
        in_specs=[pl.BlockSpec((tm, D_MODEL), lambda i: (i, 0)),
                  _const_spec((1, D_MODEL)),
                  _const_spec((D_MODEL, 2 * D_FF)),
                  _const_spec((D_FF, D_MODEL))],
        out_specs=pl.BlockSpec((tm, D_MODEL), lambda i: (i, 0)),
        out_shape=jax.ShapeDtypeStruct(x2.shape, _F32),
        compiler_params=_params("parallel"),
        name="ffn_dense",
    )(x2, g, w_gu, w_down)


def _proj_b_kernel(x_ref, g_ref, w_ref, o_ref):
    h = _rms(x_ref[...], g_ref[...]).astype(_BF16)
    proj = _dot(h, w_ref[...])
    col = lax.broadcasted_iota(jnp.int32, (1, proj.shape[1]), 1)
    is_query = (col < TOK_WIDTH) | (col >= 3 * TOK_WIDTH)
    o_ref[...] = (proj * jnp.where(is_query, ATTN_SCALE, 1.0)).astype(_BF16)


def _proj_b(x2, g, w_in, *, tm):
    t = x2.shape[0]
    n = w_in.shape[1]
    return pl.pallas_call(
        _proj_b_kernel,
        grid=(t // tm,),
        in_specs=[pl.BlockSpec((tm, D_MODEL), lambda i: (i, 0)),
                  _const_spec((1, D_MODEL)),
                  _const_spec((D_MODEL, n))],
        out_specs=pl.BlockSpec((tm, n), lambda i: (i, 0)),
        out_shape=jax.ShapeDtypeStruct((t, n), _BF16),
        compiler_params=_params("parallel"),
        name="proj_natten",
    )(x2, g, w_in)


def _natten_pattern(step, n_steps):
    return jnp.where(step < 2, step, jnp.where(step >= n_steps - 2, step - n_steps + NAT_PATTERNS, 2))


def _natten_window_start(step, rows):
    first = jnp.clip(NAT_ROWS_PER_STEP * step - WIN_H // 2, 0, rows - WIN_H)
    return jnp.minimum(first, rows - NAT_WIN_ROWS)


def _natten_kernel(q_ref, k_ref, v_ref, tab_ref, o_ref, *, rows):
    n_steps = rows // NAT_ROWS_PER_STEP
    nq = NAT_ROWS_PER_STEP * GRID_W
    low_half = lax.broadcasted_iota(jnp.int32, (1, LANES), 1) < HEAD_DIM

    def step_fn(step, carry):
        key0 = pl.multiple_of(_natten_window_start(step, rows) * GRID_W, GRID_W)
        q0 = pl.multiple_of(step * nq, nq)
        kw = k_ref[0, pl.ds(key0, NAT_KEYS), :]
        vw = v_ref[0, pl.ds(key0, NAT_KEYS), :]
        q2 = q_ref[0, pl.ds(q0, nq), :]
        zero = jnp.zeros_like(q2)
        qm = jnp.concatenate([jnp.where(low_half, q2, zero), jnp.where(low_half, zero, q2)], axis=0)
        st = lax.dot_general(kw, qm, _NT, preferred_element_type=_F32)
        st = st + tab_ref[0, _natten_pattern(step, n_steps)]
        m = jnp.max(st, axis=0, keepdims=True)
        e = jnp.exp(st - m)
        pt = (e / jnp.sum(e, axis=0, keepdims=True)).astype(_BF16)
        o2 = lax.dot_general(pt, vw, _TN, preferred_element_type=_F32)
        o_ref[0, pl.ds(q0, nq), :] = jnp.where(low_half, o2[:nq], o2[nq:]).astype(_BF16)
        return carry

    lax.fori_loop(0, n_steps, step_fn, 0)


def _natten_bias_table(rpb, rows):
    n_steps = rows // NAT_ROWS_PER_STEP
    assert rows % NAT_ROWS_PER_STEP == 0 and rows >= NAT_WIN_ROWS and n_steps >= NAT_PATTERNS
    steps = np.array([0, 1, 2, n_steps - 2, n_steps - 1])
    first = np.clip(NAT_ROWS_PER_STEP * steps - WIN_H // 2, 0, rows - WIN_H)
    win0 = np.minimum(first, rows - NAT_WIN_ROWS)
    i = np.arange(NAT_WIN_ROWS)[None, :, None, None, None]
    kc = np.arange(GRID_W)[None, None, :, None, None]
    j = np.arange(NAT_ROWS_PER_STEP)[None, None, None, :, None]
    qc = np.arange(GRID_W)[None, None, None, None, :]
    r = NAT_ROWS_PER_STEP * steps[:, None, None, None, None] + j
    key_row = win0[:, None, None, None, None] + i
    start = np.clip(r - WIN_H // 2, 0, rows - WIN_H)
    row_ok = (key_row >= start) & (key_row < start + WIN_H)
    col_start = np.clip(qc - WIN_W // 2, 0, GRID_W - WIN_W)
    col_ok = (kc >= col_start) & (kc < col_start + WIN_W)
    rel_row = np.clip(key_row - r + (WIN_H - 1), 0, 2 * WIN_H - 2)
    rel_col = np.clip(kc - qc + (WIN_W - 1), 0, 2 * WIN_W - 2)
    shape = (NAT_PATTERNS, NAT_WIN_ROWS, GRID_W, NAT_ROWS_PER_STEP, GRID_W)
    rel_row = np.broadcast_to(rel_row, shape).reshape(NAT_PATTERNS, NAT_KEYS, -1)
    rel_col = np.broadcast_to(rel_col, shape).reshape(NAT_PATTERNS, NAT_KEYS, -1)
    ok = np.broadcast_to(row_ok & col_ok, shape).reshape(NAT_PATTERNS, NAT_KEYS, -1)
    tab = rpb.astype(_F32)[:, rel_row, rel_col]
    tab = jnp.where(ok[None], tab, NEG_INF)
    nq = NAT_ROWS_PER_STEP * GRID_W
    tab = tab.reshape(N_PAIR, 2, NAT_PATTERNS, NAT_KEYS, nq)
    return tab.transpose(0, 2, 3, 1, 4).reshape(N_PAIR, NAT_PATTERNS, NAT_KEYS, 2 * nq)


def _natten(proj3, table):
    b, s, _ = proj3.shape
    rows = s // GRID_W
    blk = (1, s, LANES)
    return pl.pallas_call(
        functools.partial(_natten_kernel, rows=rows),
        grid=(N_PAIR, b),
        in_specs=[pl.BlockSpec(blk, lambda p, i: (i, 0, p)),
                  pl.BlockSpec(blk, lambda p, i: (i, 0, N_PAIR + p)),
                  pl.BlockSpec(blk, lambda p, i: (i, 0, 2 * N_PAIR + p)),
                  pl.BlockSpec((1,) + table.shape[1:], lambda p, i: (p, 0, 0, 0))],
        out_specs=pl.BlockSpec(blk, lambda p, i: (i, 0, p)),
        out_shape=jax.ShapeDtypeStruct((b, s, TOK_WIDTH), _BF16),
        compiler_params=_params("parallel", "parallel"),
        name="natten",
    )(proj3, proj3, proj3, table)


def _mixer_b_out_kernel(x_ref, tok_ref, qx_ref, k_ref, v_ref, wout_ref, o_ref):
    o_ref[0] = _xattn_outproj(x_ref[0], tok_ref[0], qx_ref[0], k_ref[0], v_ref[0], wout_ref)


def _mixer_b_out(x, tok, proj3, k, v, w_out, *, tm):
    b, s, _ = x.shape
    qx_block = 3 * TOK_WIDTH // XA_WIDTH
    return pl.pallas_call(
        _mixer_b_out_kernel,
        grid=(b, s // tm),
        in_specs=[pl.BlockSpec((1, tm, D_MODEL), lambda i, j: (i, j, 0)),
                  pl.BlockSpec((1, tm, TOK_WIDTH), lambda i, j: (i, j, 0)),
                  pl.BlockSpec((1, tm, XA_WIDTH), lambda i, j: (i, j, qx_block)),
                  pl.BlockSpec((1, N_MEM, XA_WIDTH), lambda i, j: (i, 0, 0)),
                  pl.BlockSpec((1, N_MEM, XA_WIDTH), lambda i, j: (i, 0, 0)),
                  _const_spec((D_MODEL, D_MODEL))],
        out_specs=pl.BlockSpec((1, tm, D_MODEL), lambda i, j: (i, j, 0)),
        out_shape=jax.ShapeDtypeStruct(x.shape, _F32),
        compiler_params=_params("parallel", "parallel"),
        name="mixer_natten_out",
    )(x, tok, proj3, k, v, w_out)


def _pack_halves(h):
    half = D_MODEL // 2
    hi = pltpu.bitcast(h[:, :half].astype(_BF16).astype(_F32), jnp.uint32)
    lo = pltpu.bitcast(h[:, half:].astype(_BF16).astype(_F32), jnp.uint32)
    return hi | (lo >> 16)


def _unpack_halves(w):
    hi = pltpu.bitcast(w & jnp.uint32(0xFFFF0000), _F32).astype(_BF16)
    lo = pltpu.bitcast(w << 16, _F32).astype(_BF16)
    return jnp.concatenate([hi, lo], axis=1)


def _router_kernel(x_ref, g_ref, wr_ref, h_ref, idx_ref, gate_ref):
    h = _rms(x_ref[...], g_ref[...])
    h_ref[...] = _pack_halves(h)
    logits = _dot(h, wr_ref[...])
    lane = lax.broadcasted_iota(jnp.int32, logits.shape, 1)
    lg = jnp.where(lane < N_EXPERTS, logits, -jnp.inf)
    v1 = jnp.max(lg, axis=-1, keepdims=True)
    i1 = jnp.min(jnp.where(lg == v1, lane, LANES), axis=-1, keepdims=True)
    lg2 = jnp.where(lane == i1, -jnp.inf, lg)
    v2 = jnp.max(lg2, axis=-1, keepdims=True)
    i2 = jnp.min(jnp.where(lg2 == v2, lane, LANES), axis=-1, keepdims=True)
    e2 = jnp.exp(v2 - v1)
    g1 = 1.0 / (1.0 + e2)
    g2 = e2 / (1.0 + e2)
    idx = jnp.where(lane == 0, i1, jnp.where(lane == 1, i2, 0))
    gate = jnp.where(lane == 0, g1, jnp.where(lane == 1, g2, 0.0))
    idx_ref[...] = jnp.transpose(idx.astype(_F32))[:SUBLANES].astype(jnp.int32)
    gate_ref[...] = jnp.transpose(gate)[:SUBLANES]


def _router(x2, g, w_router_pad, *, tm):
    t = x2.shape[0]
    return pl.pallas_call(
        _router_kernel,
        grid=(t // tm,),
        in_specs=[pl.BlockSpec((tm, D_MODEL), lambda i: (i, 0)),
                  _const_spec((1, D_MODEL)),
                  _const_spec((D_MODEL, LANES))],
        out_specs=[pl.BlockSpec((tm, D_MODEL // 2), lambda i: (i, 0)),
                   pl.BlockSpec((SUBLANES, tm), lambda i: (0, i)),
                   pl.BlockSpec((SUBLANES, tm), lambda i: (0, i))],
        out_shape=[jax.ShapeDtypeStruct((t, D_MODEL // 2), jnp.uint32),
                   jax.ShapeDtypeStruct((SUBLANES, t), jnp.int32),
                   jax.ShapeDtypeStruct((SUBLANES, t), _F32)],
        compiler_params=_params("parallel"),
        name="router",
    )(x2, g, w_router_pad)


def _dispatch_kernel(dest_hbm, h_ref, xs_in, xs_hbm, dest_smem, idx_sem, row_sem, *, tg):
    del xs_in
    step = pl.program_id(0)
    n = 2 * tg
    idx_copy = pltpu.make_async_copy(dest_hbm.at[step], dest_smem, idx_sem)
    idx_copy.start()
    idx_copy.wait()

    def row_copy(a):
        return pltpu.make_async_copy(h_ref.at[pl.ds(a // 2, 1)], xs_hbm.at[pl.ds(dest_smem[a], 1)], row_sem)

    def start(a, carry):
        row_copy(a).start()
        return carry

    def wait(a, carry):
        row_copy(a).wait()
        return carry

    lax.fori_loop(0, n, start, 0)
    lax.fori_loop(0, n, wait, 0)


def _dispatch(dest, h_packed, n_rows, *, tg):
    t, half = h_packed.shape
    xs0 = jnp.zeros((n_rows, half), jnp.uint32)
    return pl.pallas_call(
        functools.partial(_dispatch_kernel, tg=tg),
        grid=(t // tg,),
        in_specs=[pl.BlockSpec(memory_space=pl.ANY),
                  pl.BlockSpec((tg, half), lambda i: (i, 0)),
                  pl.BlockSpec(memory_space=pl.ANY)],
        out_specs=pl.BlockSpec(memory_space=pl.ANY),
        out_shape=jax.ShapeDtypeStruct((n_rows, half), jnp.uint32),
        scratch_shapes=[pltpu.SMEM((2 * tg,), jnp.int32),
                        pltpu.SemaphoreType.DMA(()),
                        pltpu.SemaphoreType.DMA(())],
        input_output_aliases={2: 0},
        compiler_params=_params("arbitrary"),
        name="moe_dispatch",
    )(dest.reshape(t // tg, 2 * tg), h_packed, xs0)


def _expert_kernel(be_ref, xs_ref, wgu_ref, wd_ref, ys_ref):
    del be_ref
    h = _unpack_halves(xs_ref[...])
    acc = jnp.zeros(ys_ref.shape, _F32)
    for c in range(D_FF_EXPERT // FF_CHUNK_EXPERT):
        lo = c * FF_CHUNK_EXPERT
        gate = _dot(h, wgu_ref[0, :, lo:lo + FF_CHUNK_EXPERT])
        up = _dot(h, wgu_ref[0, :, D_FF_EXPERT + lo:D_FF_EXPERT + lo + FF_CHUNK_EXPERT])
        a = (_silu(gate) * up).astype(_BF16)
        acc = acc + _dot(a, wd_ref[0, lo:lo + FF_CHUNK_EXPERT, :])
    ys_ref[...] = acc


def _experts(block_expert, xs, w_gu, w_down):
    n_rows, half = xs.shape
    grid_spec = pltpu.PrefetchScalarGridSpec(
        num_scalar_prefetch=1,
        grid=(n_rows // MOE_ROWS,),
        in_specs=[pl.BlockSpec((MOE_ROWS, half), lambda i, be: (i, 0)),
                  pl.BlockSpec((1, D_MODEL, 2 * D_FF_EXPERT), lambda i, be: (be[i], 0, 0),
                               pipeline_mode=pl.Buffered(1)),
                  pl.BlockSpec((1, D_FF_EXPERT, D_MODEL), lambda i, be: (be[i], 0, 0),
                               pipeline_mode=pl.Buffered(1))],
        out_specs=pl.BlockSpec((MOE_ROWS, D_MODEL), lambda i, be: (i, 0)),
    )
    return pl.pallas_call(
        _expert_kernel,
        grid_spec=grid_spec,
        out_shape=jax.ShapeDtypeStruct((n_rows, D_MODEL), _F32),
        compiler_params=_params("arbitrary"),
        name="moe_experts",
    )(block_expert, xs, w_gu, w_down)


def _combine_kernel(dest_hbm, x_ref, gate_ref, g_ref, ys_hbm, o_ref, dest_smem, rows, idx_sem, row_sem, *, tc):
    step = pl.program_id(0)
    n = 2 * tc
    idx_copy = pltpu.make_async_copy(dest_hbm.at[step], dest_smem, idx_sem)
    idx_copy.start()
    idx_copy.wait()

    def row_copy(a):
        return pltpu.make_async_copy(ys_hbm.at[pl.ds(dest_smem[a], 1)],
                                     rows.at[a % 2, pl.ds(a // 2, 1)], row_sem)

    def start(a, carry):
        row_copy(a).start()
        return carry

    def wait(a, carry):
        row_copy(a).wait()
        return carry

    lax.fori_loop(0, n, start, 0)
    lax.fori_loop(0, n, wait, 0)

    gates = gate_ref[...]
    pad = jnp.zeros((tc - SUBLANES, tc), _F32)
    gcol = jnp.transpose(jnp.concatenate([gates, pad], axis=0))
    f = rows[0] * gcol[:, 0:1] + rows[1] * gcol[:, 1:2]
    o_ref[...] = _rms(x_ref[...] + f, g_ref[...])


def _combine(dest, x2, gates, g_final, ys, *, tc):
    t = x2.shape[0]
    return pl.pallas_call(
        functools.partial(_combine_kernel, tc=tc),
        grid=(t // tc,),
        in_specs=[pl.BlockSpec(memory_space=pl.ANY),
                  pl.BlockSpec((tc, D_MODEL), lambda i: (i, 0)),
                  pl.BlockSpec((SUBLANES, tc), lambda i: (0, i)),
                  _const_spec((1, D_MODEL)),
                  pl.BlockSpec(memory_space=pl.ANY)],
        out_specs=pl.BlockSpec((tc, D_MODEL), lambda i: (i, 0)),
        out_shape=jax.ShapeDtypeStruct(x2.shape, _F32),
        scratch_shapes=[pltpu.SMEM((2 * tc,), jnp.int32),
                        pltpu.VMEM((2, tc, D_MODEL), _F32),
                        pltpu.SemaphoreType.DMA(()),
                        pltpu.SemaphoreType.DMA(())],
        compiler_params=_params("arbitrary"),
        name="moe_combine",
    )(dest.reshape(t // tc, 2 * tc), x2, gates, g_final, ys)


def _routing_offsets(expert_idx, n_blocks):
    t = expert_idx.shape[1]
    flat_e = expert_idx[:2].T.reshape(2 * t)
    onehot = (flat_e[:, None] == jnp.arange(N_EXPERTS)[None, :]).astype(jnp.int32)
    running = jnp.cumsum(onehot, axis=0)
    rank = jnp.sum((running - onehot) * onehot, axis=1)
    counts = running[-1]
    padded = (counts + MOE_ROWS - 1) // MOE_ROWS * MOE_ROWS
    pad_end = jnp.cumsum(padded)
    pad_start = pad_end - padded
    dest = (pad_start[flat_e] + rank).astype(jnp.int32)
    block_expert = jnp.minimum(
        jnp.searchsorted(pad_end, jnp.arange(n_blocks) * MOE_ROWS, side='right'), N_EXPERTS - 1)
    return dest, block_expert.astype(jnp.int32)


def _trunk(x, mem, w):
    b, s, d = x.shape
    t = b * s
    rows = s // GRID_W

    k0, v0 = _mem_kv(mem, w['norm_mem'][0], w['w_mem_kv'][0])
    x = _mixer_a(x, k0, v0, w['norm_mix'][0], w['w_in_a'], w['g_v_a'], w['w_s_a'], w['b_s_a'],
                 w['w_out'][0], tm=256)
    x2 = _ffn(x.reshape(t, d), w['norm_ffn'][0], w['w_ffn_gu'], w['w_ffn_down'], tm=512)

    k1, v1 = _mem_kv(mem, w['norm_mem'][1], w['w_mem_kv'][1])
    proj = _proj_b(x2, w['norm_mix'][1], w['w_in_b'], tm=512).reshape(b, s, -1)
    tok = _natten(proj, _natten_bias_table(w['rpb_b'], rows))
    x = _mixer_b_out(x2.reshape(b, s, d), tok, proj, k1, v1, w['w_out'][1], tm=512)
    x2 = x.reshape(t, d)

    h_packed, expert_idx, gates = _router(x2, w['norm_ffn'][1], w['w_router'], tm=512)
    n_blocks = 2 * t // MOE_ROWS + N_EXPERTS
    dest, block_expert = _routing_offsets(expert_idx, n_blocks)
    xs = _dispatch(dest, h_packed, n_blocks * MOE_ROWS, tg=256)
    ys = _experts(block_expert, xs, w['w_exp_gu'], w['w_exp_down'])
    y = _combine(dest, x2, gates, w['norm_final'], ys, tc=128)
    return y.reshape(b, s, d)


def _prepare_weights(norm_mix, norm_mem, norm_ffn, norm_final, w_in_a, g_v_a, w_s_a, b_s_a, w_in_b, rpb_b,
                     w_mem_kv, w_out, w_ffn_gu, w_ffn_down, w_router, w_exp_gu, w_exp_down):
    assert w_in_a.shape[0] == 1 and w_in_b.shape[0] == 1 and norm_mix.shape[0] == 2
    row = lambda a: a.astype(_F32)[:, None, :]
    return {
        'norm_mix': row(norm_mix), 'norm_mem': row(norm_mem), 'norm_ffn': row(norm_ffn),
        'norm_final': norm_final.astype(_F32)[None, :],
        'w_in_a': w_in_a[0].astype(_BF16),
        'g_v_a': g_v_a.astype(_F32),
        'w_s_a': w_s_a[0].astype(_BF16).reshape(N_PAIR, 2 * CHUNK, CHUNK),
        'b_s_a': jnp.repeat(b_s_a[0].astype(_F32).T, HEAD_DIM, axis=1),
        'w_in_b': w_in_b[0].astype(_BF16),
        'rpb_b': rpb_b[0],
        'w_mem_kv': w_mem_kv.astype(_BF16),
        'w_out': w_out.astype(_BF16),
        'w_ffn_gu': w_ffn_gu[0].astype(_BF16),
        'w_ffn_down': w_ffn_down[0].astype(_BF16),
        'w_router': jnp.pad(w_router[0].astype(_F32), ((0, 0), (0, LANES - N_EXPERTS))),
        'w_exp_gu': w_exp_gu[0].astype(_BF16),
        'w_exp_down': w_exp_down[0].astype(_BF16),
    }


def kernel(x_prompt, x_sample, mem_prompt, mem_sample, norm_mix, norm_mem, norm_ffn, norm_final, w_in_a, g_v_a, w_s_a, b_s_a, w_in_b, rpb_b, w_mem_kv, w_out, w_ffn_gu, w_ffn_down, w_router, w_exp_gu, w_exp_down):
    w = _prepare_weights(norm_mix, norm_mem, norm_ffn, norm_final, w_in_a, g_v_a, w_s_a, b_s_a, w_in_b, rpb_b,
                         w_mem_kv, w_out, w_ffn_gu, w_ffn_down, w_router, w_exp_gu, w_exp_down)
    return (_trunk(x_prompt, mem_prompt, w), _trunk(x_sample, mem_sample, w))
```

```python
import functools

import numpy as np
import jax
import jax.numpy as jnp
from jax import lax
from jax.experimental import pallas as pl
from jax.experimental.pallas import tpu as pltpu
from jax.experimental.pallas import tpu_sc as plsc

D_MODEL = 1024
HEAD_DIM = 64
XA_HEADS = 4
XA_WIDTH = XA_HEADS * HEAD_DIM
TOK_WIDTH = D_MODEL - XA_WIDTH
CHUNK = 128
N_PAIR = TOK_WIDTH // (2 * HEAD_DIM)
GRID_W = 64
WIN_H = 8
WIN_W = 16
N_MEM = 256
D_FF = 2816
N_EXPERTS = 8
D_FF_EXPERT = 3584
RMS_EPS = 1e-6
NEG_INF = -1e30
ATTN_SCALE = HEAD_DIM ** -0.5

LANES = 128
SUBLANES = 8
VMEM_LIMIT = 56 * 1024 * 1024

NAT_ROWS_PER_STEP = 2
NAT_WIN_ROWS = WIN_H + NAT_ROWS_PER_STEP - 1
NAT_KEYS = NAT_WIN_ROWS * GRID_W
NAT_PATTERNS = 5
NAT_UNROLL = 4

MOE_ROWS = 512
SC_CORES = 2
SC_SUBCORES = 16
SC_WINDOW = 64
FF_CHUNK_DENSE = 1408
FF_CHUNK_EXPERT = 512

_BF16 = jnp.bfloat16
_F32 = jnp.float32
_NT = (((1,), (1,)), ((), ()))
_TN = (((0,), (0,)), ((), ()))


def _params(*semantics):
    return pltpu.CompilerParams(dimension_semantics=semantics, vmem_limit_bytes=VMEM_LIMIT)


def _const_spec(shape):
    zeros = (0,) * len(shape)
    return pl.BlockSpec(shape, lambda *_: zeros, pipeline_mode=pl.Buffered(1))


def _rms(x, g):
    ms = jnp.mean(x * x, axis=-1, keepdims=True)
    return x * lax.rsqrt(ms + RMS_EPS) * g


def _dot(a, b):
    return jnp.dot(a, b, preferred_element_type=_F32)


def _gelu(x):
    return 0.5 * x * (1.0 + lax.erf(x * np.float32(1.0 / np.sqrt(2.0))))


def _silu(x):
    return x / (1.0 + jnp.exp(-x))


def _softmax_rows(s):
    m = jnp.max(s, axis=-1, keepdims=True)
    e = jnp.exp(s - m)
    return e / jnp.sum(e, axis=-1, keepdims=True)


def _xattn_outproj(x, tok, qx, k, v, wout_ref):
    lane = lax.broadcasted_iota(jnp.int32, (1, XA_WIDTH), 1)
    qx = qx.astype(_F32)
    xa = jnp.zeros(qx.shape, _F32)
    for h in range(XA_HEADS):
        in_head = (lane >= h * HEAD_DIM) & (lane < (h + 1) * HEAD_DIM)
        qh = jnp.where(in_head, qx, 0.0).astype(_BF16)
        s = lax.dot_general(qh, k, _NT, preferred_element_type=_F32)
        p = _softmax_rows(s).astype(_BF16)
        xa = jnp.where(in_head, _dot(p, v), xa)
    y = _dot(tok, wout_ref[:TOK_WIDTH, :]) + _dot(xa.astype(_BF16), wout_ref[TOK_WIDTH:, :])
    return x + y


def _mem_kv_kernel(mem_ref, g_ref, w_ref, k_ref, v_ref):
    m = _rms(mem_ref[0], g_ref[...]).astype(_BF16)
    kv = _dot(m, w_ref[...])
    k_ref[0] = kv[:, :XA_WIDTH].astype(_BF16)
    v_ref[0] = kv[:, XA_WIDTH:].astype(_BF16)


def _mem_kv(mem, g, w_kv):
    b = mem.shape[0]
    out = jax.ShapeDtypeStruct((b, N_MEM, XA_WIDTH), _BF16)
    return pl.pallas_call(
        _mem_kv_kernel,
        grid=(b,),
        in_specs=[pl.BlockSpec((1, N_MEM, D_MODEL), lambda i: (i, 0, 0)),
                  _const_spec((1, D_MODEL)),
                  _const_spec((D_MODEL, 2 * XA_WIDTH))],
        out_specs=[pl.BlockSpec((1, N_MEM, XA_WIDTH), lambda i: (i, 0, 0))] * 2,
        out_shape=[out, out],
        compiler_params=_params("parallel"),
        name="mem_kv",
    )(mem, g, w_kv)


def _mixer_a_kernel(x_ref, k_ref, v_ref, g_ref, win_ref, gv_ref, ws_ref, bs_ref, wout_ref, o_ref, *, tm):
    x = x_ref[0]
    h = _rms(x, g_ref[...]).astype(_BF16)
    proj = _dot(h, win_ref[...])
    z = _gelu(proj[:, :2 * TOK_WIDTH])
    u = z[:, :TOK_WIDTH]
    vn = _rms(z[:, TOK_WIDTH:], gv_ref[...]).astype(_BF16)
    low_half = lax.broadcasted_iota(jnp.int32, (1, LANES), 1) < HEAD_DIM
    bias = bs_ref[...]
    toks = []
    for c in range(tm // CHUNK):
        vc = vn[c * CHUNK:(c + 1) * CHUNK]
        cols = []
        for j in range(N_PAIR):
            r = _dot(ws_ref[j], vc[:, j * LANES:(j + 1) * LANES])
            cols.append(jnp.where(low_half, r[:CHUNK], r[CHUNK:]))
        sv = jnp.concatenate(cols, axis=1) + bias
        toks.append(u[c * CHUNK:(c + 1) * CHUNK] * sv)
    tok = jnp.concatenate(toks, axis=0).astype(_BF16)
    qx = proj[:, 2 * TOK_WIDTH:] * ATTN_SCALE
    o_ref[0] = _xattn_outproj(x, tok, qx, k_ref[0], v_ref[0], wout_ref)


def _mixer_a(x, k, v, g, w_in, g_v, w_s2, b_s2, w_out, *, tm):
    b, s, _ = x.shape
    return pl.pallas_call(
        functools.partial(_mixer_a_kernel, tm=tm),
        grid=(b, s // tm),
        in_specs=[pl.BlockSpec((1, tm, D_MODEL), lambda i, j: (i, j, 0)),
                  pl.BlockSpec((1, N_MEM, XA_WIDTH), lambda i, j: (i, 0, 0)),
                  pl.BlockSpec((1, N_MEM, XA_WIDTH), lambda i, j: (i, 0, 0)),
                  _const_spec((1, D_MODEL)),
                  _const_spec((D_MODEL, 2 * TOK_WIDTH + XA_WIDTH)),
                  _const_spec((1, TOK_WIDTH)),
                  _const_spec((N_PAIR, 2 * CHUNK, CHUNK)),
                  _const_spec((CHUNK, TOK_WIDTH)),
                  _const_spec((D_MODEL, D_MODEL))],
        out_specs=pl.BlockSpec((1, tm, D_MODEL), lambda i, j: (i, j, 0)),
        out_shape=jax.ShapeDtypeStruct(x.shape, _F32),
        compiler_params=_params("parallel", "parallel"),
        name="mixer_gmlp",
    )(x, k, v, g, w_in, g_v, w_s2, b_s2, w_out)


def _ffn_kernel(x_ref, g_ref, wgu_ref, wd_ref, o_ref):
    x = x_ref[...]
    h = _rms(x, g_ref[...]).astype(_BF16)
    acc = x
    for c in range(D_FF // FF_CHUNK_DENSE):
        lo = c * FF_CHUNK_DENSE
        gate = _dot(h, wgu_ref[:, lo:lo + FF_CHUNK_DENSE])
        up = _dot(h, wgu_ref[:, D_FF + lo:D_FF + lo + FF_CHUNK_DENSE])
        a = (_silu(gate) * up).astype(_BF16)
        acc = acc + _dot(a, wd_ref[lo:lo + FF_CHUNK_DENSE, :])
    o_ref[...] = acc


def _ffn(x2, g, w_gu, w_down, *, tm):
    t = x2.shape[0]
    return pl.pallas_call(
        _ffn_kernel,
        grid=(t // tm,),
        in_specs=[pl.BlockSpec((tm, D_MODEL), lambda i: (i, 0)),
                  _const_spec((1, D_MODEL)),
                  _const_spec((D_MODEL, 2 * D_FF)),
                  _const_spec((D_FF, D_MODEL))],
        out_specs=pl.BlockSpec((tm, D_MODEL), lambda i: (i, 0)),
        out_shape=jax.ShapeDtypeStruct(x2.shape, _F32),
        compiler_params=_params("parallel"),
        name="ffn_dense",
    )(x2, g, w_gu, w_down)


def _proj_b_kernel(x_ref, g_ref, w_ref, o_ref):
    h = _rms(x_ref[...], g_ref[...]).astype(_BF16)
    proj = _dot(h, w_ref[...])
    col = lax.broadcasted_iota(jnp.int32, (1, proj.shape[1]), 1)
    is_query = (col < TOK_WIDTH) | (col >= 3 * TOK_WIDTH)
    o_ref[...] = (proj * jnp.where(is_query, ATTN_SCALE, 1.0)).astype(_BF16)


def _proj_b(x2, g, w_in, *, tm):
    t = x2.shape[0]
    n = w_in.shape[1]
    return pl.pallas_call(
        _proj_b_kernel,
        grid=(t // tm,),
        in_specs=[pl.BlockSpec((tm, D_MODEL), lambda i: (i, 0)),
                  _const_spec((1, D_MODEL)),
                  _const_spec((D_MODEL, n))],
        out_specs=pl.BlockSpec((tm, n), lambda i: (i, 0)),
        out_shape=jax.ShapeDtypeStruct((t, n), _BF16),
        compiler_params=_params("parallel"),
        name="proj_natten",
    )(x2, g, w_in)


def _natten_pattern(step, n_steps):
    return jnp.where(step < 2, step, jnp.where(step >= n_steps - 2, step - n_steps + NAT_PATTERNS, 2))


def _natten_window_start(step, rows):
    first = jnp.clip(NAT_ROWS_PER_STEP * step - WIN_H // 2, 0, rows - WIN_H)
    return jnp.minimum(first, rows - NAT_WIN_ROWS)


def _natten_kernel(q_ref, k_ref, v_ref, tab_ref, o_ref, *, rows):
    n_steps = rows // NAT_ROWS_PER_STEP
    nq = NAT_ROWS_PER_STEP * GRID_W
    low_half = lax.broadcasted_iota(jnp.int32, (1, LANES), 1) < HEAD_DIM

    ones = jnp.ones((NAT_KEYS, LANES), _BF16)

    def one_step(step):
        key0 = pl.multiple_of(_natten_window_start(step, rows) * GRID_W, GRID_W)
        q0 = pl.multiple_of(step * nq, nq)
        kw = k_ref[0, pl.ds(key0, NAT_KEYS), :]
        vw = v_ref[0, pl.ds(key0, NAT_KEYS), :]
        q2 = q_ref[0, pl.ds(q0, nq), :]
        zero = jnp.zeros_like(q2)
        qm = jnp.concatenate([jnp.where(low_half, q2, zero), jnp.where(low_half, zero, q2)], axis=0)
        st = lax.dot_general(kw, qm, _NT, preferred_element_type=_F32)
        st = st + tab_ref[0, _natten_pattern(step, n_steps)]
        e = jnp.exp(st - jnp.max(st, axis=0, keepdims=True)).astype(_BF16)
        o2 = lax.dot_general(e, jnp.concatenate([vw, ones], axis=1), _TN,
                             preferred_element_type=_F32)
        o2 = o2[:, :LANES] / o2[:, LANES:]
        o_ref[0, pl.ds(q0, nq), :] = jnp.where(low_half, o2[:nq], o2[nq:]).astype(_BF16)

    def trip(i, carry):
        for u in range(NAT_UNROLL):
            one_step(i * NAT_UNROLL + u)
        return carry

    assert n_steps % NAT_UNROLL == 0
    lax.fori_loop(0, n_steps // NAT_UNROLL, trip, 0)


def _natten_bias_kernel(picked_ref, expand_ref, o_ref):
    o_ref[0] = jnp.dot(picked_ref[0], expand_ref[...], preferred_element_type=_F32,
                       precision=lax.Precision.HIGHEST)


def _natten_bias_table(rpb, rows):
    n_steps = rows // NAT_ROWS_PER_STEP
    assert rows % NAT_ROWS_PER_STEP == 0 and rows >= NAT_WIN_ROWS and n_steps >= NAT_PATTERNS
    steps = np.array([0, 1, 2, n_steps - 2, n_steps - 1])
    first = np.clip(NAT_ROWS_PER_STEP * steps - WIN_H // 2, 0, rows - WIN_H)
    win0 = np.minimum(first, rows - NAT_WIN_ROWS)
    i = np.arange(NAT_WIN_ROWS)[None, :, None, None, None]
    kc = np.arange(GRID_W)[None, None, :, None, None]
    j = np.arange(NAT_ROWS_PER_STEP)[None, None, None, :, None]
    qc = np.arange(GRID_W)[None, None, None, None, :]
    r = NAT_ROWS_PER_STEP * steps[:, None, None, None, None] + j
    key_row = win0[:, None, None, None, None] + i
    start = np.clip(r - WIN_H // 2, 0, rows - WIN_H)
    row_ok = (key_row >= start) & (key_row < start + WIN_H)
    col_start = np.clip(qc - WIN_W // 2, 0, GRID_W - WIN_W)
    col_ok = (kc >= col_start) & (kc < col_start + WIN_W)
    rel_row = np.clip(key_row - r + (WIN_H - 1), 0, 2 * WIN_H - 2)
    rel_col = np.clip(kc - qc + (WIN_W - 1), 0, 2 * WIN_W - 2)
    shape = (NAT_PATTERNS, NAT_WIN_ROWS, GRID_W, NAT_ROWS_PER_STEP, GRID_W)
    ok = np.broadcast_to(row_ok & col_ok, shape)
    n_heads, n_rel_col = rpb.shape[0], rpb.shape[2]
    row_sel = rel_row.reshape(-1)
    n_sel = -(-row_sel.size // SUBLANES) * SUBLANES
    picked = jnp.zeros((n_heads, n_sel, LANES), _F32)
    picked = picked.at[:, :row_sel.size, :n_rel_col].set(rpb.astype(_F32)[:, row_sel, :])
    expand = np.zeros((LANES, GRID_W * GRID_W), np.float32)
    expand[rel_col.reshape(-1), np.arange(GRID_W * GRID_W)] = 1.0
    tab = pl.pallas_call(
        _natten_bias_kernel,
        grid=(n_heads,),
        in_specs=[pl.BlockSpec((1, n_sel, LANES), lambda h: (h, 0, 0)),
                  _const_spec(expand.shape)],
        out_specs=pl.BlockSpec((1, n_sel, GRID_W * GRID_W), lambda h: (h, 0, 0)),
        out_shape=jax.ShapeDtypeStruct((n_heads, n_sel, GRID_W * GRID_W), _F32),
        compiler_params=_params("parallel"),
        name="natten_bias",
    )(picked, jnp.asarray(expand))
    nq = NAT_ROWS_PER_STEP * GRID_W
    tab = tab[:, :row_sel.size].reshape(n_heads, NAT_PATTERNS, NAT_WIN_ROWS, NAT_ROWS_PER_STEP, GRID_W, GRID_W)
    tab = tab.transpose(0, 1, 2, 4, 3, 5)
    tab = jnp.where(ok[None], tab, NEG_INF).reshape(N_PAIR, 2, NAT_PATTERNS, NAT_KEYS, nq)
    return tab.transpose(0, 2, 3, 1, 4).reshape(N_PAIR, NAT_PATTERNS, NAT_KEYS, 2 * nq)


def _natten(proj3, table):
    b, s, _ = proj3.shape
    rows = s // GRID_W
    blk = (1, s, LANES)
    return pl.pallas_call(
        functools.partial(_natten_kernel, rows=rows),
        grid=(N_PAIR, b),
        in_specs=[pl.BlockSpec(blk, lambda p, i: (i, 0, p)),
                  pl.BlockSpec(blk, lambda p, i: (i, 0, N_PAIR + p)),
                  pl.BlockSpec(blk, lambda p, i: (i, 0, 2 * N_PAIR + p)),
                  pl.BlockSpec((1,) + table.shape[1:], lambda p, i: (p, 0, 0, 0))],
        out_specs=pl.BlockSpec(blk, lambda p, i: (i, 0, p)),
        out_shape=jax.ShapeDtypeStruct((b, s, TOK_WIDTH), _BF16),
        compiler_params=_params("parallel", "parallel"),
        name="natten",
    )(proj3, proj3, proj3, table)


def _mixer_b_out_kernel(x_ref, tok_ref, qx_ref, k_ref, v_ref, wout_ref, o_ref):
    o_ref[0] = _xattn_outproj(x_ref[0], tok_ref[0], qx_ref[0], k_ref[0], v_ref[0], wout_ref)


def _mixer_b_out(x, tok, proj3, k, v, w_out, *, tm):
    b, s, _ = x.shape
    qx_block = 3 * TOK_WIDTH // XA_WIDTH
    return pl.pallas_call(
        _mixer_b_out_kernel,
        grid=(b, s // tm),
        in_specs=[pl.BlockSpec((1, tm, D_MODEL), lambda i, j: (i, j, 0)),
                  pl.BlockSpec((1, tm, TOK_WIDTH), lambda i, j: (i, j, 0)),
                  pl.BlockSpec((1, tm, XA_WIDTH), lambda i, j: (i, j, qx_block)),
                  pl.BlockSpec((1, N_MEM, XA_WIDTH), lambda i, j: (i, 0, 0)),
                  pl.BlockSpec((1, N_MEM, XA_WIDTH), lambda i, j: (i, 0, 0)),
                  _const_spec((D_MODEL, D_MODEL))],
        out_specs=pl.BlockSpec((1, tm, D_MODEL), lambda i, j: (i, j, 0)),
        out_shape=jax.ShapeDtypeStruct(x.shape, _F32),
        compiler_params=_params("parallel", "parallel"),
        name="mixer_natten_out",
    )(x, tok, proj3, k, v, w_out)


def _pack_halves(h):
    half = D_MODEL // 2
    hi = pltpu.bitcast(h[:, :half].astype(_BF16).astype(_F32), jnp.uint32)
    lo = pltpu.bitcast(h[:, half:].astype(_BF16).astype(_F32), jnp.uint32)
    return hi | (lo >> 16)


def _unpack_halves(w):
    hi = pltpu.bitcast(w & jnp.uint32(0xFFFF0000), _F32).astype(_BF16)
    lo = pltpu.bitcast(w << 16, _F32).astype(_BF16)
    return jnp.concatenate([hi, lo], axis=1)


def _router_kernel(x_ref, g_ref, wr_ref, h_ref, idx_ref, gate_ref):
    h = _rms(x_ref[...], g_ref[...])
    h_ref[...] = _pack_halves(h)
    logits = _dot(h, wr_ref[...])
    lane = lax.broadcasted_iota(jnp.int32, logits.shape, 1)
    lg = jnp.where(lane < N_EXPERTS, logits, -jnp.inf)
    v1 = jnp.max(lg, axis=-1, keepdims=True)
    i1 = jnp.min(jnp.where(lg == v1, lane, LANES), axis=-1, keepdims=True)
    lg2 = jnp.where(lane == i1, -jnp.inf, lg)
    v2 = jnp.max(lg2, axis=-1, keepdims=True)
    i2 = jnp.min(jnp.where(lg2 == v2, lane, LANES), axis=-1, keepdims=True)
    e2 = jnp.exp(v2 - v1)
    g1 = 1.0 / (1.0 + e2)
    g2 = e2 / (1.0 + e2)
    idx = jnp.where(lane == 0, i1, jnp.where(lane == 1, i2, 0))
    gate = jnp.where(lane == 0, g1, jnp.where(lane == 1, g2, 0.0))
    idx_ref[...] = jnp.transpose(idx.astype(_F32))[:SUBLANES].astype(jnp.int32)
    gate_ref[...] = jnp.transpose(gate)[:SUBLANES]


def _router(x2, g, w_router_pad, *, tm):
    t = x2.shape[0]
    return pl.pallas_call(
        _router_kernel,
        grid=(t // tm,),
        in_specs=[pl.BlockSpec((tm, D_MODEL), lambda i: (i, 0)),
                  _const_spec((1, D_MODEL)),
                  _const_spec((D_MODEL, LANES))],
        out_specs=[pl.BlockSpec((tm, D_MODEL // 2), lambda i: (i, 0)),
                   pl.BlockSpec((SUBLANES, tm), lambda i: (0, i)),
                   pl.BlockSpec((SUBLANES, tm), lambda i: (0, i))],
        out_shape=[jax.ShapeDtypeStruct((t, D_MODEL // 2), jnp.uint32),
                   jax.ShapeDtypeStruct((SUBLANES, t), jnp.int32),
                   jax.ShapeDtypeStruct((SUBLANES, t), _F32)],
        compiler_params=_params("parallel"),
        name="router",
    )(x2, g, w_router_pad)


def _gather_rows(table, idx):
    n, d = idx.shape[0], table.shape[1]
    workers = SC_CORES * SC_SUBCORES
    per_worker = n // workers
    pairs = per_worker // (2 * SC_WINDOW)
    assert n == workers * pairs * 2 * SC_WINDOW, (n, workers, SC_WINDOW)
    mesh = plsc.VectorSubcoreMesh(core_axis_name="core", subcore_axis_name="subcore",
                                  num_cores=SC_CORES, num_subcores=SC_SUBCORES)

    def gather_kernel(table_hbm, idx_hbm, out_hbm, idx_a, idx_b, rows_a, rows_b, sem_a, sem_b):
        worker = lax.axis_index("subcore") * SC_CORES + lax.axis_index("core")
        base = worker * per_worker

        @pl.loop(0, pairs)
        def _(p):
            r0 = base + p * (2 * SC_WINDOW)
            r1 = r0 + SC_WINDOW
            pltpu.sync_copy(idx_hbm.at[pl.ds(r0, SC_WINDOW)], idx_a)
            gather_a = pltpu.async_copy(table_hbm.at[idx_a], rows_a, sem_a)
            pltpu.sync_copy(idx_hbm.at[pl.ds(r1, SC_WINDOW)], idx_b)
            gather_b = pltpu.async_copy(table_hbm.at[idx_b], rows_b, sem_b)
            gather_a.wait()
            pltpu.sync_copy(rows_a, out_hbm.at[pl.ds(r0, SC_WINDOW)])
            gather_b.wait()
            pltpu.sync_copy(rows_b, out_hbm.at[pl.ds(r1, SC_WINDOW)])

    idx_buf = pltpu.VMEM((SC_WINDOW,), jnp.int32)
    row_buf = pltpu.VMEM((SC_WINDOW, d), table.dtype)
    return pl.kernel(gather_kernel, out_type=jax.ShapeDtypeStruct((n, d), table.dtype), mesh=mesh,
                     scratch_types=[idx_buf, idx_buf, row_buf, row_buf,
                                    pltpu.SemaphoreType.DMA, pltpu.SemaphoreType.DMA],
                     name="moe_row_gather")(table, idx)


def _expert_kernel(meta_ref, xs_ref, wgu_ref, wd_ref, ys_ref, *, n_blocks):
    @pl.when(pl.program_id(0) < meta_ref[n_blocks])
    def _():
        h = _unpack_halves(xs_ref[...])
        acc = jnp.zeros((MOE_ROWS, D_MODEL), _F32)
        for c in range(D_FF_EXPERT // FF_CHUNK_EXPERT):
            lo = c * FF_CHUNK_EXPERT
            gate = _dot(h, wgu_ref[0, :, lo:lo + FF_CHUNK_EXPERT])
            up = _dot(h, wgu_ref[0, :, D_FF_EXPERT + lo:D_FF_EXPERT + lo + FF_CHUNK_EXPERT])
            a = (_silu(gate) * up).astype(_BF16)
            acc = acc + _dot(a, wd_ref[0, lo:lo + FF_CHUNK_EXPERT, :])
        ys_ref[...] = _pack_halves(acc)


def _experts(meta, xs, w_gu, w_down):
    n_rows, half = xs.shape
    n_blocks = n_rows // MOE_ROWS
    grid_spec = pltpu.PrefetchScalarGridSpec(
        num_scalar_prefetch=1,
        grid=(n_blocks,),
        in_specs=[pl.BlockSpec((MOE_ROWS, half), lambda i, m: (i, 0)),
                  pl.BlockSpec((1, D_MODEL, 2 * D_FF_EXPERT), lambda i, m: (m[i], 0, 0),
                               pipeline_mode=pl.Buffered(1)),
                  pl.BlockSpec((1, D_FF_EXPERT, D_MODEL), lambda i, m: (m[i], 0, 0),
                               pipeline_mode=pl.Buffered(1))],
        out_specs=pl.BlockSpec((MOE_ROWS, half), lambda i, m: (i, 0)),
    )
    return pl.pallas_call(
        functools.partial(_expert_kernel, n_blocks=n_blocks),
        grid_spec=grid_spec,
        out_shape=jax.ShapeDtypeStruct((n_rows, half), jnp.uint32),
        compiler_params=_params("arbitrary"),
        name="moe_experts",
    )(meta, xs, w_gu, w_down)


def _combine_kernel(x_ref, y0_ref, y1_ref, gate_ref, g_ref, o_ref, *, tc):
    gates = gate_ref[...]
    pad = jnp.zeros((LANES - SUBLANES, LANES), _F32)
    cols = [jnp.transpose(jnp.concatenate([gates[:, c * LANES:(c + 1) * LANES], pad], axis=0))
            for c in range(tc // LANES)]
    gcol = jnp.concatenate(cols, axis=0)
    y0 = _unpack_halves(y0_ref[...]).astype(_F32)
    y1 = _unpack_halves(y1_ref[...]).astype(_F32)
    f = y0 * gcol[:, 0:1] + y1 * gcol[:, 1:2]
    o_ref[...] = _rms(x_ref[...] + f, g_ref[...])


def _combine(x2, y0, y1, gates, g_final, *, tc):
    t = x2.shape[0]
    half = y0.shape[1]
    return pl.pallas_call(
        functools.partial(_combine_kernel, tc=tc),
        grid=(t // tc,),
        in_specs=[pl.BlockSpec((tc, D_MODEL), lambda i: (i, 0)),
                  pl.BlockSpec((tc, half), lambda i: (i, 0)),
                  pl.BlockSpec((tc, half), lambda i: (i, 0)),
                  pl.BlockSpec((SUBLANES, tc), lambda i: (0, i)),
                  _const_spec((1, D_MODEL))],
        out_specs=pl.BlockSpec((tc, D_MODEL), lambda i: (i, 0)),
        out_shape=jax.ShapeDtypeStruct(x2.shape, _F32),
        compiler_params=_params("parallel"),
        name="moe_combine",
    )(x2, y0, y1, gates, g_final)


def _routing_offsets(expert_idx, n_blocks):
    t = expert_idx.shape[1]
    n = 2 * t
    flat_e = expert_idx[:2].reshape(n)
    experts = jnp.arange(N_EXPERTS, dtype=jnp.int32)
    onehot = (flat_e[:, None] == experts[None, :]).astype(jnp.int32)
    running = jnp.cumsum(onehot, axis=0)
    rank = jnp.sum((running - onehot) * onehot, axis=1)
    counts = running[-1]
    padded = (counts + MOE_ROWS - 1) // MOE_ROWS * MOE_ROWS
    pad_end = jnp.cumsum(padded)
    pad_start = pad_end - padded
    start = jnp.cumsum(counts) - counts
    dest = (jnp.sum(onehot * pad_start[None, :], axis=1) + rank).astype(jnp.int32)

    block_row = jnp.arange(n_blocks, dtype=jnp.int32) * MOE_ROWS
    block_expert = jnp.minimum(jnp.sum((block_row[:, None] >= pad_end[None, :]).astype(jnp.int32), axis=1),
                               N_EXPERTS - 1)
    meta = jnp.concatenate([block_expert, pad_end[-1:] // MOE_ROWS]).astype(jnp.int32)

    order = jnp.argsort(flat_e, stable=True).astype(jnp.int32)
    row_onehot = jnp.repeat((block_expert[:, None] == experts[None, :]).astype(jnp.int32), MOE_ROWS, axis=0)
    off = jnp.arange(n_blocks * MOE_ROWS, dtype=jnp.int32) - jnp.sum(row_onehot * pad_start[None, :], axis=1)
    valid = off < jnp.sum(row_onehot * counts[None, :], axis=1)
    pos = jnp.clip(jnp.sum(row_onehot * start[None, :], axis=1) + off, 0, n - 1)
    src_tok = jnp.where(valid, order[pos] % t, 0).astype(jnp.int32)
    return dest, src_tok, meta


def _trunk(x, mem, w):
    b, s, d = x.shape
    t = b * s

    k0, v0 = _mem_kv(mem, w['norm_mem'][0], w['w_mem_kv'][0])
    x = _mixer_a(x, k0, v0, w['norm_mix'][0], w['w_in_a'], w['g_v_a'], w['w_s_a'], w['b_s_a'],
                 w['w_out'][0], tm=256)
    x2 = _ffn(x.reshape(t, d), w['norm_ffn'][0], w['w_ffn_gu'], w['w_ffn_down'], tm=512)

    k1, v1 = _mem_kv(mem, w['norm_mem'][1], w['w_mem_kv'][1])
    proj = _proj_b(x2, w['norm_mix'][1], w['w_in_b'], tm=512).reshape(b, s, -1)
    tok = _natten(proj, _natten_bias_table(w['rpb_b'], s // GRID_W))
    x = _mixer_b_out(x2.reshape(b, s, d), tok, proj, k1, v1, w['w_out'][1], tm=512)
    x2 = x.reshape(t, d)

    h_packed, expert_idx, gates = _router(x2, w['norm_ffn'][1], w['w_router'], tm=512)
    n_blocks = 2 * t // MOE_ROWS + N_EXPERTS
    dest, src_tok, meta = _routing_offsets(expert_idx, n_blocks)
    xs = _gather_rows(h_packed, src_tok)
    ys = _experts(meta, xs, w['w_exp_gu'], w['w_exp_down'])
    y0 = _gather_rows(ys, dest[:t])
    y1 = _gather_rows(ys, dest[t:])
    y = _combine(x2, y0, y1, gates, w['norm_final'], tc=256)
    return y.reshape(b, s, d)


def _prepare_weights(norm_mix, norm_mem, norm_ffn, norm_final, w_in_a, g_v_a, w_s_a, b_s_a, w_in_b, rpb_b,
                     w_mem_kv, w_out, w_ffn_gu, w_ffn_down, w_router, w_exp_gu, w_exp_down):
    assert w_in_a.shape[0] == 1 and w_in_b.shape[0] == 1 and norm_mix.shape[0] == 2
    row = lambda a: a.astype(_F32)[:, None, :]
    return {
        'norm_mix': row(norm_mix), 'norm_mem': row(norm_mem), 'norm_ffn': row(norm_ffn),
        'norm_final': norm_final.astype(_F32)[None, :],
        'w_in_a': w_in_a[0].astype(_BF16),
        'g_v_a': g_v_a.astype(_F32),
        'w_s_a': w_s_a[0].astype(_BF16).reshape(N_PAIR, 2 * CHUNK, CHUNK),
        'b_s_a': jnp.repeat(b_s_a[0].astype(_F32).T, HEAD_DIM, axis=1),
        'w_in_b': w_in_b[0].astype(_BF16),
        'rpb_b': rpb_b[0],
        'w_mem_kv': w_mem_kv.astype(_BF16),
        'w_out': w_out.astype(_BF16),
        'w_ffn_gu': w_ffn_gu[0].astype(_BF16),
        'w_ffn_down': w_ffn_down[0].astype(_BF16),
        'w_router': jnp.pad(w_router[0].astype(_F32), ((0, 0), (0, LANES - N_EXPERTS))),
        'w_exp_gu': w_exp_gu[0].astype(_BF16),
        'w_exp_down': w_exp_down[0].astype(_BF16),
    }


def kernel(x_prompt, x_sample, mem_prompt, mem_sample, norm_mix, norm_mem, norm_ffn, norm_final, w_in_a, g_v_a, w_s_a, b_s_a, w_in_b, rpb_b, w_mem_kv, w_out, w_ffn_gu, w_ffn_down, w_router, w_exp_gu, w_exp_down):
    w = _prepare_weights(norm_mix, norm_mem, norm_ffn, norm_final, w_in_a, g_v_a, w_s_a, b_s_a, w_in_b, rpb_b,
                         w_mem_kv, w_out, w_ffn_gu, w_ffn_down, w_router, w_exp_gu, w_exp_down)
    return (_trunk(x_prompt, mem_prompt, w), _trunk(x_sample, mem_sample, w))
```

```python
import functools

import numpy as np
import jax
import jax.numpy as jnp
from jax import lax
from jax.experimental import pallas as pl
from jax.experimental.pallas import tpu as pltpu
from jax.experimental.pallas import tpu_sc as plsc

D_MODEL = 1024
HEAD_DIM = 64
XA_HEADS = 4
XA_WIDTH = XA_HEADS * HEAD_DIM
TOK_WIDTH = D_MODEL - XA_WIDTH
CHUNK = 128
N_PAIR = TOK_WIDTH // (2 * HEAD_DIM)
GRID_W = 64
WIN_H = 8
WIN_W = 16
N_MEM = 256
D_FF = 2816
N_EXPERTS = 8
D_FF_EXPERT = 3584
RMS_EPS = 1e-6
NEG_INF = -1e30
ATTN_SCALE = HEAD_DIM ** -0.5

LANES = 128
SUBLANES = 8
VMEM_LIMIT = 56 * 1024 * 1024

NAT_ROWS_PER_STEP = 2
NAT_WIN_ROWS = WIN_H + NAT_ROWS_PER_STEP - 1
NAT_KEYS = NAT_WIN_ROWS * GRID_W
NAT_PATTERNS = 5
NAT_UNROLL = 4

MOE_ROWS = 512
SC_CORES = 2
SC_SUBCORES = 16
SC_WINDOW = 64
MXU_WIDTH = 256
FF_SPLIT_DENSE = 6 * MXU_WIDTH
ROW_TILE = 512
FF_CHUNK_EXPERT = 512

_BF16 = jnp.bfloat16
_F32 = jnp.float32
_NT = (((1,), (1,)), ((), ()))
_TN = (((0,), (0,)), ((), ()))


def _params(*semantics):
    return pltpu.CompilerParams(dimension_semantics=semantics, vmem_limit_bytes=VMEM_LIMIT)


def _const_spec(shape):
    zeros = (0,) * len(shape)
    return pl.BlockSpec(shape, lambda *_: zeros, pipeline_mode=pl.Buffered(1))


def _rms(x, g):
    ms = jnp.mean(x * x, axis=-1, keepdims=True)
    return x * lax.rsqrt(ms + RMS_EPS) * g


def _dot(a, b):
    return jnp.dot(a, b, preferred_element_type=_F32)


def _gelu(x):
    return 0.5 * x * (1.0 + lax.erf(x * np.float32(1.0 / np.sqrt(2.0))))


def _silu(x):
    return x / (1.0 + jnp.exp(-x))


def _softmax_rows(s):
    m = jnp.max(s, axis=-1, keepdims=True)
    e = jnp.exp(s - m)
    return e / jnp.sum(e, axis=-1, keepdims=True)


def _xattn_outproj(x, tok, qx, k, v, wout_ref):
    lane = lax.broadcasted_iota(jnp.int32, (1, XA_WIDTH), 1)
    qx = qx.astype(_F32)
    xa = jnp.zeros(qx.shape, _F32)
    for h in range(XA_HEADS):
        in_head = (lane >= h * HEAD_DIM) & (lane < (h + 1) * HEAD_DIM)
        qh = jnp.where(in_head, qx, 0.0).astype(_BF16)
        s = lax.dot_general(qh, k, _NT, preferred_element_type=_F32)
        p = _softmax_rows(s).astype(_BF16)
        xa = jnp.where(in_head, _dot(p, v), xa)
    y = _dot(tok, wout_ref[:TOK_WIDTH, :]) + _dot(xa.astype(_BF16), wout_ref[TOK_WIDTH:, :])
    return x + y


def _mem_kv_kernel(mem_ref, g_ref, w_ref, k_ref, v_ref):
    m = _rms(mem_ref[0], g_ref[...]).astype(_BF16)
    kv = _dot(m, w_ref[...])
    k_ref[0] = kv[:, :XA_WIDTH].astype(_BF16)
    v_ref[0] = kv[:, XA_WIDTH:].astype(_BF16)


def _mem_kv(mem, g, w_kv):
    b = mem.shape[0]
    out = jax.ShapeDtypeStruct((b, N_MEM, XA_WIDTH), _BF16)
    return pl.pallas_call(
        _mem_kv_kernel,
        grid=(b,),
        in_specs=[pl.BlockSpec((1, N_MEM, D_MODEL), lambda i: (i, 0, 0)),
                  _const_spec((1, D_MODEL)),
                  _const_spec((D_MODEL, 2 * XA_WIDTH))],
        out_specs=[pl.BlockSpec((1, N_MEM, XA_WIDTH), lambda i: (i, 0, 0))] * 2,
        out_shape=[out, out],
        compiler_params=_params("parallel"),
        name="mem_kv",
    )(mem, g, w_kv)


def _mixer_a_kernel(x_ref, k_ref, v_ref, g_ref, win_ref, gv_ref, ws_ref, bs_ref, wout_ref, o_ref, *, tm):
    assert tm % (2 * CHUNK) == 0
    x = x_ref[0]
    h = _rms(x, g_ref[...]).astype(_BF16)
    proj = _dot(h, win_ref[...])
    z = _gelu(proj[:, :2 * TOK_WIDTH])
    u = z[:, :TOK_WIDTH]
    vn = _rms(z[:, TOK_WIDTH:], gv_ref[...]).astype(_BF16)
    low_half = lax.broadcasted_iota(jnp.int32, (1, LANES), 1) < HEAD_DIM
    bias = bs_ref[...]
    zero = jnp.zeros((CHUNK, LANES), _BF16)
    toks = []
    for c in range(0, tm // CHUNK, 2):
        va, vb = vn[c * CHUNK:(c + 1) * CHUNK], vn[(c + 1) * CHUNK:(c + 2) * CHUNK]
        cols_a, cols_b = [], []
        for j in range(N_PAIR):
            pa, pb = va[:, j * LANES:(j + 1) * LANES], vb[:, j * LANES:(j + 1) * LANES]
            even = jnp.concatenate([jnp.where(low_half, pa, zero), jnp.where(low_half, pb, zero)], axis=1)
            odd = jnp.concatenate([jnp.where(low_half, zero, pa), jnp.where(low_half, zero, pb)], axis=1)
            r = _dot(ws_ref[j], jnp.concatenate([even, odd], axis=0))
            cols_a.append(r[:, :LANES])
            cols_b.append(r[:, LANES:])
        for side, cols in enumerate((cols_a, cols_b)):
            sv = jnp.concatenate(cols, axis=1) + bias
            toks.append(u[(c + side) * CHUNK:(c + side + 1) * CHUNK] * sv)
    tok = jnp.concatenate(toks, axis=0).astype(_BF16)
    qx = proj[:, 2 * TOK_WIDTH:] * ATTN_SCALE
    o_ref[0] = _xattn_outproj(x, tok, qx, k_ref[0], v_ref[0], wout_ref)


def _mixer_a(x, k, v, g, w_in, g_v, w_s2, b_s2, w_out, *, tm):
    b, s, _ = x.shape
    return pl.pallas_call(
        functools.partial(_mixer_a_kernel, tm=tm),
        grid=(b, s // tm),
        in_specs=[pl.BlockSpec((1, tm, D_MODEL), lambda i, j: (i, j, 0)),
                  pl.BlockSpec((1, N_MEM, XA_WIDTH), lambda i, j: (i, 0, 0)),
                  pl.BlockSpec((1, N_MEM, XA_WIDTH), lambda i, j: (i, 0, 0)),
                  _const_spec((1, D_MODEL)),
                  _const_spec((D_MODEL, 2 * TOK_WIDTH + XA_WIDTH)),
                  _const_spec((1, TOK_WIDTH)),
                  _const_spec((N_PAIR, CHUNK, 2 * CHUNK)),
                  _const_spec((CHUNK, TOK_WIDTH)),
                  _const_spec((D_MODEL, D_MODEL))],
        out_specs=pl.BlockSpec((1, tm, D_MODEL), lambda i, j: (i, j, 0)),
        out_shape=jax.ShapeDtypeStruct(x.shape, _F32),
        compiler_params=_params("parallel", "parallel"),
        name="mixer_gmlp",
    )(x, k, v, g, w_in, g_v, w_s2, b_s2, w_out)


def _ffn_kernel(x_ref, g_ref, wgu_ref, wd_ref, o_ref):
    x = x_ref[...]
    h = _rms(x, g_ref[...]).astype(_BF16)
    acc = x
    for lo, hi in ((0, FF_SPLIT_DENSE), (FF_SPLIT_DENSE, D_FF)):
        gate = _dot(h, wgu_ref[:, lo:hi])
        up = _dot(h, wgu_ref[:, D_FF + lo:D_FF + hi])
        a = (_silu(gate) * up).astype(_BF16)
        acc = acc + _dot(a, wd_ref[lo:hi, :])
    o_ref[...] = acc


def _ffn(x2, g, w_gu, w_down, *, tm):
    t = x2.shape[0]
    return pl.pallas_call(
        _ffn_kernel,
        grid=(t // tm,),
        in_specs=[pl.BlockSpec((tm, D_MODEL), lambda i: (i, 0)),
                  _const_spec((1, D_MODEL)),
                  _const_spec((D_MODEL, 2 * D_FF)),
                  _const_spec((D_FF, D_MODEL))],
        out_specs=pl.BlockSpec((tm, D_MODEL), lambda i: (i, 0)),
        out_shape=jax.ShapeDtypeStruct(x2.shape, _F32),
        compiler_params=_params("parallel"),
        name="ffn_dense",
    )(x2, g, w_gu, w_down)


def _proj_b_kernel(x_ref, g_ref, w_ref, o_ref):
    h = _rms(x_ref[...], g_ref[...]).astype(_BF16)
    proj = _dot(h, w_ref[...])
    col = lax.broadcasted_iota(jnp.int32, (1, proj.shape[1]), 1)
    is_query = (col < TOK_WIDTH) | (col >= 3 * TOK_WIDTH)
    o_ref[...] = (proj * jnp.where(is_query, ATTN_SCALE, 1.0)).astype(_BF16)


def _proj_b(x2, g, w_in, *, tm):
    t = x2.shape[0]
    n = w_in.shape[1]
    return pl.pallas_call(
        _proj_b_kernel,
        grid=(t // tm,),
        in_specs=[pl.BlockSpec((tm, D_MODEL), lambda i: (i, 0)),
                  _const_spec((1, D_MODEL)),
                  _const_spec((D_MODEL, n))],
        out_specs=pl.BlockSpec((tm, n), lambda i: (i, 0)),
        out_shape=jax.ShapeDtypeStruct((t, n), _BF16),
        compiler_params=_params("parallel"),
        name="proj_natten",
    )(x2, g, w_in)


def _natten_pattern(step, n_steps):
    return jnp.where(step < 2, step, jnp.where(step >= n_steps - 2, step - n_steps + NAT_PATTERNS, 2))


def _natten_window_start(step, rows):
    first = jnp.clip(NAT_ROWS_PER_STEP * step - WIN_H // 2, 0, rows - WIN_H)
    return jnp.minimum(first, rows - NAT_WIN_ROWS)


def _natten_kernel(q_ref, k_ref, v_ref, tab_ref, o_ref, *, rows):
    n_steps = rows // NAT_ROWS_PER_STEP
    nq = NAT_ROWS_PER_STEP * GRID_W
    low_half = lax.broadcasted_iota(jnp.int32, (1, LANES), 1) < HEAD_DIM

    ones = jnp.ones((NAT_KEYS, LANES), _BF16)

    def one_step(step):
        key0 = pl.multiple_of(_natten_window_start(step, rows) * GRID_W, GRID_W)
        q0 = pl.multiple_of(step * nq, nq)
        kw = k_ref[0, pl.ds(key0, NAT_KEYS), :]
        vw = v_ref[0, pl.ds(key0, NAT_KEYS), :]
        q2 = q_ref[0, pl.ds(q0, nq), :]
        zero = jnp.zeros_like(q2)
        qm = jnp.concatenate([jnp.where(low_half, q2, zero), jnp.where(low_half, zero, q2)], axis=0)
        st = lax.dot_general(kw, qm, _NT, preferred_element_type=_F32)
        st = st + tab_ref[0, _natten_pattern(step, n_steps)]
        e = jnp.exp(st - jnp.max(st, axis=0, keepdims=True)).astype(_BF16)
        o2 = lax.dot_general(e, jnp.concatenate([vw, ones], axis=1), _TN,
                             preferred_element_type=_F32)
        o2 = o2[:, :LANES] / o2[:, LANES:]
        o_ref[0, pl.ds(q0, nq), :] = jnp.where(low_half, o2[:nq], o2[nq:]).astype(_BF16)

    def trip(i, carry):
        for u in range(NAT_UNROLL):
            one_step(i * NAT_UNROLL + u)
        return carry

    assert n_steps % NAT_UNROLL == 0
    lax.fori_loop(0, n_steps // NAT_UNROLL, trip, 0)


def _natten_bias_kernel(picked_ref, expand_ref, o_ref):
    o_ref[0] = jnp.dot(picked_ref[0], expand_ref[...], preferred_element_type=_F32,
                       precision=lax.Precision.HIGHEST)


def _natten_bias_table(rpb, rows):
    n_steps = rows // NAT_ROWS_PER_STEP
    assert rows % NAT_ROWS_PER_STEP == 0 and rows >= NAT_WIN_ROWS and n_steps >= NAT_PATTERNS
    steps = np.array([0, 1, 2, n_steps - 2, n_steps - 1])
    first = np.clip(NAT_ROWS_PER_STEP * steps - WIN_H // 2, 0, rows - WIN_H)
    win0 = np.minimum(first, rows - NAT_WIN_ROWS)
    i = np.arange(NAT_WIN_ROWS)[None, :, None, None, None]
    kc = np.arange(GRID_W)[None, None, :, None, None]
    j = np.arange(NAT_ROWS_PER_STEP)[None, None, None, :, None]
    qc = np.arange(GRID_W)[None, None, None, None, :]
    r = NAT_ROWS_PER_STEP * steps[:, None, None, None, None] + j
    key_row = win0[:, None, None, None, None] + i
    start = np.clip(r - WIN_H // 2, 0, rows - WIN_H)
    row_ok = (key_row >= start) & (key_row < start + WIN_H)
    col_start = np.clip(qc - WIN_W // 2, 0, GRID_W - WIN_W)
    col_ok = (kc >= col_start) & (kc < col_start + WIN_W)
    rel_row = np.clip(key_row - r + (WIN_H - 1), 0, 2 * WIN_H - 2)
    rel_col = np.clip(kc - qc + (WIN_W - 1), 0, 2 * WIN_W - 2)
    shape = (NAT_PATTERNS, NAT_WIN_ROWS, GRID_W, NAT_ROWS_PER_STEP, GRID_W)
    ok = np.broadcast_to(row_ok & col_ok, shape)
    n_heads, n_rel_col = rpb.shape[0], rpb.shape[2]
    row_sel = rel_row.reshape(-1)
    n_sel = -(-row_sel.size // SUBLANES) * SUBLANES
    picked = jnp.zeros((n_heads, n_sel, LANES), _F32)
    picked = picked.at[:, :row_sel.size, :n_rel_col].set(rpb.astype(_F32)[:, row_sel, :])
    expand = np.zeros((LANES, GRID_W * GRID_W), np.float32)
    expand[rel_col.reshape(-1), np.arange(GRID_W * GRID_W)] = 1.0
    tab = pl.pallas_call(
        _natten_bias_kernel,
        grid=(n_heads,),
        in_specs=[pl.BlockSpec((1, n_sel, LANES), lambda h: (h, 0, 0)),
                  _const_spec(expand.shape)],
        out_specs=pl.BlockSpec((1, n_sel, GRID_W * GRID_W), lambda h: (h, 0, 0)),
        out_shape=jax.ShapeDtypeStruct((n_heads, n_sel, GRID_W * GRID_W), _F32),
        compiler_params=_params("parallel"),
        name="natten_bias",
    )(picked, jnp.asarray(expand))
    nq = NAT_ROWS_PER_STEP * GRID_W
    tab = tab[:, :row_sel.size].reshape(n_heads, NAT_PATTERNS, NAT_WIN_ROWS, NAT_ROWS_PER_STEP, GRID_W, GRID_W)
    tab = tab.transpose(0, 1, 2, 4, 3, 5)
    tab = jnp.where(ok[None], tab, NEG_INF).reshape(N_PAIR, 2, NAT_PATTERNS, NAT_KEYS, nq)
    return tab.transpose(0, 2, 3, 1, 4).reshape(N_PAIR, NAT_PATTERNS, NAT_KEYS, 2 * nq)


def _natten(proj3, table):
    b, s, _ = proj3.shape
    rows = s // GRID_W
    blk = (1, s, LANES)
    return pl.pallas_call(
        functools.partial(_natten_kernel, rows=rows),
        grid=(N_PAIR, b),
        in_specs=[pl.BlockSpec(blk, lambda p, i: (i, 0, p)),
                  pl.BlockSpec(blk, lambda p, i: (i, 0, N_PAIR + p)),
                  pl.BlockSpec(blk, lambda p, i: (i, 0, 2 * N_PAIR + p)),
                  pl.BlockSpec((1,) + table.shape[1:], lambda p, i: (p, 0, 0, 0))],
        out_specs=pl.BlockSpec(blk, lambda p, i: (i, 0, p)),
        out_shape=jax.ShapeDtypeStruct((b, s, TOK_WIDTH), _BF16),
        compiler_params=_params("parallel", "parallel"),
        name="natten",
    )(proj3, proj3, proj3, table)


def _mixer_b_out_kernel(x_ref, tok_ref, qx_ref, k_ref, v_ref, wout_ref, g_ref, wr_ref,
                        o_ref, h_ref, idx_ref, gate_ref):
    x = _xattn_outproj(x_ref[0], tok_ref[0], qx_ref[0], k_ref[0], v_ref[0], wout_ref)
    o_ref[0] = x
    h = _rms(x, g_ref[...])
    h_ref[...] = _pack_halves(h)
    idx_ref[...], gate_ref[...] = _route(h, wr_ref)


def _mixer_b_out(x, tok, proj3, k, v, w_out, g_ffn, w_router2, *, tm):
    b, s, _ = x.shape
    t = b * s
    nj = s // tm
    qx_block = 3 * TOK_WIDTH // XA_WIDTH
    return pl.pallas_call(
        _mixer_b_out_kernel,
        grid=(b, nj),
        in_specs=[pl.BlockSpec((1, tm, D_MODEL), lambda i, j: (i, j, 0)),
                  pl.BlockSpec((1, tm, TOK_WIDTH), lambda i, j: (i, j, 0)),
                  pl.BlockSpec((1, tm, XA_WIDTH), lambda i, j: (i, j, qx_block)),
                  pl.BlockSpec((1, N_MEM, XA_WIDTH), lambda i, j: (i, 0, 0)),
                  pl.BlockSpec((1, N_MEM, XA_WIDTH), lambda i, j: (i, 0, 0)),
                  _const_spec((D_MODEL, D_MODEL)),
                  _const_spec((1, D_MODEL)),
                  _const_spec((D_MODEL, 2 * LANES))],
        out_specs=[pl.BlockSpec((1, tm, D_MODEL), lambda i, j: (i, j, 0)),
                   pl.BlockSpec((tm, D_MODEL // 2), lambda i, j: (i * nj + j, 0)),
                   pl.BlockSpec((SUBLANES, tm), lambda i, j: (0, i * nj + j)),
                   pl.BlockSpec((SUBLANES, tm), lambda i, j: (0, i * nj + j))],
        out_shape=[jax.ShapeDtypeStruct(x.shape, _F32),
                   jax.ShapeDtypeStruct((t, D_MODEL // 2), jnp.uint32),
                   jax.ShapeDtypeStruct((SUBLANES, t), jnp.int32),
                   jax.ShapeDtypeStruct((SUBLANES, t), _F32)],
        compiler_params=_params("parallel", "parallel"),
        name="mixer_natten_out",
    )(x, tok, proj3, k, v, w_out, g_ffn, w_router2)


def _pack_halves(h):
    half = D_MODEL // 2
    hi = pltpu.bitcast(h[:, :half].astype(_BF16).astype(_F32), jnp.uint32)
    lo = pltpu.bitcast(h[:, half:].astype(_BF16).astype(_F32), jnp.uint32)
    return hi | (lo >> 16)


def _unpack_halves(w):
    hi = pltpu.bitcast(w & jnp.uint32(0xFFFF0000), _F32).astype(_BF16)
    lo = pltpu.bitcast(w << 16, _F32).astype(_BF16)
    return jnp.concatenate([hi, lo], axis=1)


def _route(h, wr_ref):
    h_head = h.astype(_BF16)
    h_rest = (h - h_head.astype(_F32)).astype(_BF16)
    both = _dot(h_head, wr_ref[...])
    logits = both[:, :LANES] + both[:, LANES:] + _dot(h_rest, wr_ref[:, :LANES])
    assert N_EXPERTS == SUBLANES
    lt = jnp.concatenate([jnp.transpose(logits[c * LANES:(c + 1) * LANES])[:SUBLANES]
                          for c in range(h.shape[0] // LANES)], axis=1)
    expert = lax.broadcasted_iota(jnp.int32, lt.shape, 0)
    v1 = jnp.max(lt, axis=0, keepdims=True)
    i1 = jnp.min(jnp.where(lt == v1, expert, N_EXPERTS), axis=0, keepdims=True)
    lt2 = jnp.where(expert == i1, -jnp.inf, lt)
    v2 = jnp.max(lt2, axis=0, keepdims=True)
    i2 = jnp.min(jnp.where(lt2 == v2, expert, N_EXPERTS), axis=0, keepdims=True)
    e2 = jnp.exp(v2 - v1)
    g1 = 1.0 / (1.0 + e2)
    g2 = e2 / (1.0 + e2)
    idx = jnp.where(expert == 0, i1, jnp.where(expert == 1, i2, 0))
    gate = jnp.where(expert == 0, g1, jnp.where(expert == 1, g2, 0.0))
    return idx, gate


def _gather_rows(table, idx):
    n, d = idx.shape[0], table.shape[1]
    workers = SC_CORES * SC_SUBCORES
    per_worker = n // workers
    pairs = per_worker // (2 * SC_WINDOW)
    assert n == workers * pairs * 2 * SC_WINDOW, (n, workers, SC_WINDOW)
    mesh = plsc.VectorSubcoreMesh(core_axis_name="core", subcore_axis_name="subcore",
                                  num_cores=SC_CORES, num_subcores=SC_SUBCORES)

    def gather_kernel(table_hbm, idx_hbm, out_hbm, idx_a, idx_b, rows_a, rows_b, sem_a, sem_b):
        worker = lax.axis_index("subcore") * SC_CORES + lax.axis_index("core")
        base = worker * per_worker

        @pl.loop(0, pairs)
        def _(p):
            r0 = base + p * (2 * SC_WINDOW)
            r1 = r0 + SC_WINDOW
            pltpu.sync_copy(idx_hbm.at[pl.ds(r0, SC_WINDOW)], idx_a)
            gather_a = pltpu.async_copy(table_hbm.at[idx_a], rows_a, sem_a)
            pltpu.sync_copy(idx_hbm.at[pl.ds(r1, SC_WINDOW)], idx_b)
            gather_b = pltpu.async_copy(table_hbm.at[idx_b], rows_b, sem_b)
            gather_a.wait()
            pltpu.sync_copy(rows_a, out_hbm.at[pl.ds(r0, SC_WINDOW)])
            gather_b.wait()
            pltpu.sync_copy(rows_b, out_hbm.at[pl.ds(r1, SC_WINDOW)])

    idx_buf = pltpu.VMEM((SC_WINDOW,), jnp.int32)
    row_buf = pltpu.VMEM((SC_WINDOW, d), table.dtype)
    return pl.kernel(gather_kernel, out_type=jax.ShapeDtypeStruct((n, d), table.dtype), mesh=mesh,
                     scratch_types=[idx_buf, idx_buf, row_buf, row_buf,
                                    pltpu.SemaphoreType.DMA, pltpu.SemaphoreType.DMA],
                     name="moe_row_gather")(table, idx)


def _expert_kernel(meta_ref, xs_ref, wgu_ref, wd_ref, ys_ref, *, n_blocks):
    @pl.when(pl.program_id(0) < meta_ref[n_blocks])
    def _():
        h = _unpack_halves(xs_ref[...])
        acc = jnp.zeros((MOE_ROWS, D_MODEL), _F32)
        for c in range(D_FF_EXPERT // FF_CHUNK_EXPERT):
            lo = c * FF_CHUNK_EXPERT
            gate = _dot(h, wgu_ref[0, :, lo:lo + FF_CHUNK_EXPERT])
            up = _dot(h, wgu_ref[0, :, D_FF_EXPERT + lo:D_FF_EXPERT + lo + FF_CHUNK_EXPERT])
            a = (_silu(gate) * up).astype(_BF16)
            acc = acc + _dot(a, wd_ref[0, lo:lo + FF_CHUNK_EXPERT, :])
        ys_ref[...] = _pack_halves(acc)


def _experts(meta, xs, w_gu, w_down):
    n_rows, half = xs.shape
    n_blocks = n_rows // MOE_ROWS
    grid_spec = pltpu.PrefetchScalarGridSpec(
        num_scalar_prefetch=1,
        grid=(n_blocks,),
        in_specs=[pl.BlockSpec((MOE_ROWS, half), lambda i, m: (i, 0)),
                  pl.BlockSpec((1, D_MODEL, 2 * D_FF_EXPERT), lambda i, m: (m[i], 0, 0),
                               pipeline_mode=pl.Buffered(1)),
                  pl.BlockSpec((1, D_FF_EXPERT, D_MODEL), lambda i, m: (m[i], 0, 0),
                               pipeline_mode=pl.Buffered(1))],
        out_specs=pl.BlockSpec((MOE_ROWS, half), lambda i, m: (i, 0)),
    )
    return pl.pallas_call(
        functools.partial(_expert_kernel, n_blocks=n_blocks),
        grid_spec=grid_spec,
        out_shape=jax.ShapeDtypeStruct((n_rows, half), jnp.uint32),
        compiler_params=_params("arbitrary"),
        name="moe_experts",
    )(meta, xs, w_gu, w_down)


def _combine_kernel(x_ref, y0_ref, y1_ref, gate_ref, g_ref, o_ref, *, tc):
    gates = gate_ref[...]
    pad = jnp.zeros((LANES - SUBLANES, LANES), _F32)
    cols = [jnp.transpose(jnp.concatenate([gates[:, c * LANES:(c + 1) * LANES], pad], axis=0))
            for c in range(tc // LANES)]
    gcol = jnp.concatenate(cols, axis=0)
    y0 = _unpack_halves(y0_ref[...]).astype(_F32)
    y1 = _unpack_halves(y1_ref[...]).astype(_F32)
    f = y0 * gcol[:, 0:1] + y1 * gcol[:, 1:2]
    o_ref[...] = _rms(x_ref[...] + f, g_ref[...])


def _combine(x2, y0, y1, gates, g_final, *, tc):
    t = x2.shape[0]
    half = y0.shape[1]
    return pl.pallas_call(
        functools.partial(_combine_kernel, tc=tc),
        grid=(t // tc,),
        in_specs=[pl.BlockSpec((tc, D_MODEL), lambda i: (i, 0)),
                  pl.BlockSpec((tc, half), lambda i: (i, 0)),
                  pl.BlockSpec((tc, half), lambda i: (i, 0)),
                  pl.BlockSpec((SUBLANES, tc), lambda i: (0, i)),
                  _const_spec((1, D_MODEL))],
        out_specs=pl.BlockSpec((tc, D_MODEL), lambda i: (i, 0)),
        out_shape=jax.ShapeDtypeStruct(x2.shape, _F32),
        compiler_params=_params("parallel"),
        name="moe_combine",
    )(x2, y0, y1, gates, g_final)


def _routing_offsets(expert_idx, n_blocks):
    t = expert_idx.shape[1]
    n = 2 * t
    flat_e = expert_idx[:2].reshape(n)
    experts = jnp.arange(N_EXPERTS, dtype=jnp.int32)
    onehot = (flat_e[:, None] == experts[None, :]).astype(jnp.int32)
    running = jnp.cumsum(onehot, axis=0)
    rank = jnp.sum((running - onehot) * onehot, axis=1)
    counts = running[-1]
    padded = (counts + MOE_ROWS - 1) // MOE_ROWS * MOE_ROWS
    pad_end = jnp.cumsum(padded)
    pad_start = pad_end - padded
    start = jnp.cumsum(counts) - counts
    dest = (jnp.sum(onehot * pad_start[None, :], axis=1) + rank).astype(jnp.int32)

    block_row = jnp.arange(n_blocks, dtype=jnp.int32) * MOE_ROWS
    block_expert = jnp.minimum(jnp.sum((block_row[:, None] >= pad_end[None, :]).astype(jnp.int32), axis=1),
                               N_EXPERTS - 1)
    meta = jnp.concatenate([block_expert, pad_end[-1:] // MOE_ROWS]).astype(jnp.int32)

    order = jnp.argsort(flat_e, stable=True).astype(jnp.int32)
    row_onehot = jnp.repeat((block_expert[:, None] == experts[None, :]).astype(jnp.int32), MOE_ROWS, axis=0)
    off = jnp.arange(n_blocks * MOE_ROWS, dtype=jnp.int32) - jnp.sum(row_onehot * pad_start[None, :], axis=1)
    valid = off < jnp.sum(row_onehot * counts[None, :], axis=1)
    pos = jnp.clip(jnp.sum(row_onehot * start[None, :], axis=1) + off, 0, n - 1)
    src_tok = jnp.where(valid, order[pos] % t, 0).astype(jnp.int32)
    return dest, src_tok, meta


def _trunk(x, mem, w):
    b, s, d = x.shape
    t = b * s

    k0, v0 = _mem_kv(mem, w['norm_mem'][0], w['w_mem_kv'][0])
    x = _mixer_a(x, k0, v0, w['norm_mix'][0], w['w_in_a'], w['g_v_a'], w['w_s_a'], w['b_s_a'],
                 w['w_out'][0], tm=ROW_TILE)
    x2 = _ffn(x.reshape(t, d), w['norm_ffn'][0], w['w_ffn_gu'], w['w_ffn_down'], tm=ROW_TILE)

    k1, v1 = _mem_kv(mem, w['norm_mem'][1], w['w_mem_kv'][1])
    proj = _proj_b(x2, w['norm_mix'][1], w['w_in_b'], tm=ROW_TILE).reshape(b, s, -1)
    tok = _natten(proj, _natten_bias_table(w['rpb_b'], s // GRID_W))
    x, h_packed, expert_idx, gates = _mixer_b_out(x2.reshape(b, s, d), tok, proj, k1, v1, w['w_out'][1],
                                                  w['norm_ffn'][1], w['w_router'], tm=ROW_TILE)
    x2 = x.reshape(t, d)

    n_blocks = 2 * t // MOE_ROWS + N_EXPERTS
    dest, src_tok, meta = _routing_offsets(expert_idx, n_blocks)
    xs = _gather_rows(h_packed, src_tok)
    ys = _experts(meta, xs, w['w_exp_gu'], w['w_exp_down'])
    y0 = _gather_rows(ys, dest[:t])
    y1 = _gather_rows(ys, dest[t:])
    y = _combine(x2, y0, y1, gates, w['norm_final'], tc=ROW_TILE)
    return y.reshape(b, s, d)


def _cast_kernel(w_ref, o_ref):
    o_ref[...] = w_ref[...].astype(o_ref.dtype)


def _to_bf16(w, rows_per_step):
    n, r, c = w.shape
    assert r % rows_per_step == 0
    block = (1, rows_per_step, c)
    return pl.pallas_call(
        _cast_kernel,
        grid=(n, r // rows_per_step),
        in_specs=[pl.BlockSpec(block, lambda i, j: (i, j, 0))],
        out_specs=pl.BlockSpec(block, lambda i, j: (i, j, 0)),
        out_shape=jax.ShapeDtypeStruct(w.shape, _BF16),
        compiler_params=_params("parallel", "parallel"),
        name="weight_cast",
    )(w)


def _split_router_weight(w):
    head = w.astype(_BF16)
    rest = (w - head.astype(_F32)).astype(_BF16)
    pad = ((0, 0), (0, LANES - N_EXPERTS))
    return jnp.concatenate([jnp.pad(head, pad), jnp.pad(rest, pad)], axis=1)


def _prepare_weights(norm_mix, norm_mem, norm_ffn, norm_final, w_in_a, g_v_a, w_s_a, b_s_a, w_in_b, rpb_b,
                     w_mem_kv, w_out, w_ffn_gu, w_ffn_down, w_router, w_exp_gu, w_exp_down):
    assert w_in_a.shape[0] == 1 and w_in_b.shape[0] == 1 and norm_mix.shape[0] == 2
    row = lambda a: a.astype(_F32)[:, None, :]
    return {
        'norm_mix': row(norm_mix), 'norm_mem': row(norm_mem), 'norm_ffn': row(norm_ffn),
        'norm_final': norm_final.astype(_F32)[None, :],
        'w_in_a': w_in_a[0].astype(_BF16),
        'g_v_a': g_v_a.astype(_F32),
        'w_s_a': jnp.concatenate([w_s_a[0, 0::2], w_s_a[0, 1::2]], axis=2).astype(_BF16),
        'b_s_a': jnp.repeat(b_s_a[0].astype(_F32).T, HEAD_DIM, axis=1),
        'w_in_b': w_in_b[0].astype(_BF16),
        'rpb_b': rpb_b[0],
        'w_mem_kv': w_mem_kv.astype(_BF16),
        'w_out': w_out.astype(_BF16),
        'w_ffn_gu': _to_bf16(w_ffn_gu, 256)[0],
        'w_ffn_down': _to_bf16(w_ffn_down, 704)[0],
        'w_router': _split_router_weight(w_router[0].astype(_F32)),
        'w_exp_gu': _to_bf16(w_exp_gu[0], 256),
        'w_exp_down': _to_bf16(w_exp_down[0], 896),
    }


def kernel(x_prompt, x_sample, mem_prompt, mem_sample, norm_mix, norm_mem, norm_ffn, norm_final, w_in_a, g_v_a, w_s_a, b_s_a, w_in_b, rpb_b, w_mem_kv, w_out, w_ffn_gu, w_ffn_down, w_router, w_exp_gu, w_exp_down):
    w = _prepare_weights(norm_mix, norm_mem, norm_ffn, norm_final, w_in_a, g_v_a, w_s_a, b_s_a, w_in_b, rpb_b,
                         w_mem_kv, w_out, w_ffn_gu, w_ffn_down, w_router, w_exp_gu, w_exp_down)
    return (_trunk(x_prompt, mem_prompt, w), _trunk(x_sample, mem_sample, w))
```

```python
import functools

import numpy as np
import jax
import jax.numpy as jnp
from jax import lax
from jax.experimental import pallas as pl
from jax.experimental.pallas import tpu as pltpu
from jax.experimental.pallas import tpu_sc as plsc

D_MODEL = 1024
HEAD_DIM = 64
XA_HEADS = 4
XA_WIDTH = XA_HEADS * HEAD_DIM
TOK_WIDTH = D_MODEL - XA_WIDTH
CHUNK = 128
N_PAIR = TOK_WIDTH // (2 * HEAD_DIM)
GRID_W = 64
WIN_H = 8
WIN_W = 16
N_MEM = 256
D_FF = 2816
N_EXPERTS = 8
D_FF_EXPERT = 3584
RMS_EPS = 1e-6
NEG_INF = -1e30
ATTN_SCALE = HEAD_DIM ** -0.5

LANES = 128
SUBLANES = 8
VMEM_LIMIT = 56 * 1024 * 1024

NAT_ROWS_PER_STEP = 2
NAT_WIN_ROWS = WIN_H + NAT_ROWS_PER_STEP - 1
NAT_KEYS = NAT_WIN_ROWS * GRID_W
NAT_PATTERNS = 5
NAT_UNROLL = 4

MOE_ROWS = 512
SC_CORES = 2
SC_SUBCORES = 16
SC_WINDOW = 64
MXU_WIDTH = 256
FF_SPLIT_DENSE = 6 * MXU_WIDTH
ROW_TILE = 512
FF_CHUNK_EXPERT = 512

_BF16 = jnp.bfloat16
_F32 = jnp.float32
_NT = (((1,), (1,)), ((), ()))
_TN = (((0,), (0,)), ((), ()))


def _params(*semantics):
    return pltpu.CompilerParams(dimension_semantics=semantics, vmem_limit_bytes=VMEM_LIMIT)


def _const_spec(shape):
    zeros = (0,) * len(shape)
    return pl.BlockSpec(shape, lambda *_: zeros, pipeline_mode=pl.Buffered(1))


def _rms(x, g):
    ms = jnp.mean(x * x, axis=-1, keepdims=True)
    return x * lax.rsqrt(ms + RMS_EPS) * g


def _dot(a, b):
    return jnp.dot(a, b, preferred_element_type=_F32)


def _gelu(x):
    return 0.5 * x * (1.0 + lax.erf(x * np.float32(1.0 / np.sqrt(2.0))))


def _silu(x):
    return x / (1.0 + jnp.exp(-x))


def _softmax_rows(s):
    m = jnp.max(s, axis=-1, keepdims=True)
    e = jnp.exp(s - m)
    return e / jnp.sum(e, axis=-1, keepdims=True)


def _xattn_outproj(x, tok, qx, k, v, wout_ref):
    lane = lax.broadcasted_iota(jnp.int32, (1, XA_WIDTH), 1)
    qx = qx.astype(_F32)
    xa = jnp.zeros(qx.shape, _F32)
    for h in range(XA_HEADS):
        in_head = (lane >= h * HEAD_DIM) & (lane < (h + 1) * HEAD_DIM)
        qh = jnp.where(in_head, qx, 0.0).astype(_BF16)
        s = lax.dot_general(qh, k, _NT, preferred_element_type=_F32)
        p = _softmax_rows(s).astype(_BF16)
        xa = jnp.where(in_head, _dot(p, v), xa)
    y = _dot(tok, wout_ref[:TOK_WIDTH, :]) + _dot(xa.astype(_BF16), wout_ref[TOK_WIDTH:, :])
    return x + y


def _mem_kv_kernel(mem_ref, g_ref, w_ref, k_ref, v_ref):
    m = _rms(mem_ref[0], g_ref[...]).astype(_BF16)
    kv = _dot(m, w_ref[...])
    k_ref[0] = kv[:, :XA_WIDTH].astype(_BF16)
    v_ref[0] = kv[:, XA_WIDTH:].astype(_BF16)


def _mem_kv(mem, g, w_kv):
    b = mem.shape[0]
    out = jax.ShapeDtypeStruct((b, N_MEM, XA_WIDTH), _BF16)
    return pl.pallas_call(
        _mem_kv_kernel,
        grid=(b,),
        in_specs=[pl.BlockSpec((1, N_MEM, D_MODEL), lambda i: (i, 0, 0)),
                  _const_spec((1, D_MODEL)),
                  _const_spec((D_MODEL, 2 * XA_WIDTH))],
        out_specs=[pl.BlockSpec((1, N_MEM, XA_WIDTH), lambda i: (i, 0, 0))] * 2,
        out_shape=[out, out],
        compiler_params=_params("parallel"),
        name="mem_kv",
    )(mem, g, w_kv)


def _mixer_a_kernel(x_ref, k_ref, v_ref, g_ref, win_ref, gv_ref, ws_ref, bs_ref, wout_ref, o_ref, *, tm):
    assert tm % (2 * CHUNK) == 0
    x = x_ref[0]
    h = _rms(x, g_ref[...]).astype(_BF16)
    proj = _dot(h, win_ref[...])
    z = _gelu(proj[:, :2 * TOK_WIDTH])
    u = z[:, :TOK_WIDTH]
    vn = _rms(z[:, TOK_WIDTH:], gv_ref[...]).astype(_BF16)
    low_half = lax.broadcasted_iota(jnp.int32, (1, LANES), 1) < HEAD_DIM
    bias = bs_ref[...]
    zero = jnp.zeros((CHUNK, LANES), _BF16)
    toks = []
    for c in range(0, tm // CHUNK, 2):
        va, vb = vn[c * CHUNK:(c + 1) * CHUNK], vn[(c + 1) * CHUNK:(c + 2) * CHUNK]
        cols_a, cols_b = [], []
        for j in range(N_PAIR):
            pa, pb = va[:, j * LANES:(j + 1) * LANES], vb[:, j * LANES:(j + 1) * LANES]
            even = jnp.concatenate([jnp.where(low_half, pa, zero), jnp.where(low_half, pb, zero)], axis=1)
            odd = jnp.concatenate([jnp.where(low_half, zero, pa), jnp.where(low_half, zero, pb)], axis=1)
            r = _dot(ws_ref[j], jnp.concatenate([even, odd], axis=0))
            cols_a.append(r[:, :LANES])
            cols_b.append(r[:, LANES:])
        for side, cols in enumerate((cols_a, cols_b)):
            sv = jnp.concatenate(cols, axis=1) + bias
            toks.append(u[(c + side) * CHUNK:(c + side + 1) * CHUNK] * sv)
    tok = jnp.concatenate(toks, axis=0).astype(_BF16)
    qx = proj[:, 2 * TOK_WIDTH:] * ATTN_SCALE
    o_ref[0] = _xattn_outproj(x, tok, qx, k_ref[0], v_ref[0], wout_ref)


def _mixer_a(x, k, v, g, w_in, g_v, w_s2, b_s2, w_out, *, tm):
    b, s, _ = x.shape
    return pl.pallas_call(
        functools.partial(_mixer_a_kernel, tm=tm),
        grid=(b, s // tm),
        in_specs=[pl.BlockSpec((1, tm, D_MODEL), lambda i, j: (i, j, 0)),
                  pl.BlockSpec((1, N_MEM, XA_WIDTH), lambda i, j: (i, 0, 0)),
                  pl.BlockSpec((1, N_MEM, XA_WIDTH), lambda i, j: (i, 0, 0)),
                  _const_spec((1, D_MODEL)),
                  _const_spec((D_MODEL, 2 * TOK_WIDTH + XA_WIDTH)),
                  _const_spec((1, TOK_WIDTH)),
                  _const_spec((N_PAIR, CHUNK, 2 * CHUNK)),
                  _const_spec((CHUNK, TOK_WIDTH)),
                  _const_spec((D_MODEL, D_MODEL))],
        out_specs=pl.BlockSpec((1, tm, D_MODEL), lambda i, j: (i, j, 0)),
        out_shape=jax.ShapeDtypeStruct(x.shape, _F32),
        compiler_params=_params("parallel", "parallel"),
        name="mixer_gmlp",
    )(x, k, v, g, w_in, g_v, w_s2, b_s2, w_out)


def _ffn_kernel(x_ref, g_ref, wgu_ref, wd_ref, *refs):
    n_side = (len(refs) - 1) // 2
    o_ref = refs[n_side]
    x = x_ref[...]
    h = _rms(x, g_ref[...]).astype(_BF16)
    acc = x
    for lo, hi in ((0, FF_SPLIT_DENSE), (FF_SPLIT_DENSE, D_FF)):
        gate = _dot(h, wgu_ref[:, lo:hi])
        up = _dot(h, wgu_ref[:, D_FF + lo:D_FF + hi])
        a = (_silu(gate) * up).astype(_BF16)
        acc = acc + _dot(a, wd_ref[lo:hi, :])
    o_ref[...] = acc
    for src, dst in zip(refs[:n_side], refs[n_side + 1:]):
        dst[...] = src[...].astype(dst.dtype)


def _ffn(x2, g, w_gu, w_down, side_casts=(), *, tm):
    t = x2.shape[0]
    steps = t // tm
    flat = [w.reshape(-1, w.shape[-1]) for w in side_casts]
    slabs = [(w.shape[0] // steps, w.shape[1]) for w in flat]
    assert all(w.shape[0] == s[0] * steps and s[0] % (2 * SUBLANES) == 0 for w, s in zip(flat, slabs))
    side_specs = [pl.BlockSpec(s, lambda i: (i, 0)) for s in slabs]
    outs = pl.pallas_call(
        _ffn_kernel,
        grid=(steps,),
        in_specs=[pl.BlockSpec((tm, D_MODEL), lambda i: (i, 0)),
                  _const_spec((1, D_MODEL)),
                  _const_spec((D_MODEL, 2 * D_FF)),
                  _const_spec((D_FF, D_MODEL))] + side_specs,
        out_specs=[pl.BlockSpec((tm, D_MODEL), lambda i: (i, 0))] + side_specs,
        out_shape=[jax.ShapeDtypeStruct(x2.shape, _F32)] + [jax.ShapeDtypeStruct(w.shape, _BF16) for w in flat],
        compiler_params=_params("parallel"),
        name="ffn_dense",
    )(x2, g, w_gu, w_down, *flat)
    return outs[0], [o.reshape(w.shape) for o, w in zip(outs[1:], side_casts)]


def _proj_b_kernel(x_ref, g_ref, w_ref, o_ref):
    h = _rms(x_ref[...], g_ref[...]).astype(_BF16)
    proj = _dot(h, w_ref[...])
    col = lax.broadcasted_iota(jnp.int32, (1, proj.shape[1]), 1)
    is_query = (col < TOK_WIDTH) | (col >= 3 * TOK_WIDTH)
    o_ref[...] = (proj * jnp.where(is_query, ATTN_SCALE, 1.0)).astype(_BF16)


def _proj_b(x2, g, w_in, *, tm):
    t = x2.shape[0]
    n = w_in.shape[1]
    return pl.pallas_call(
        _proj_b_kernel,
        grid=(t // tm,),
        in_specs=[pl.BlockSpec((tm, D_MODEL), lambda i: (i, 0)),
                  _const_spec((1, D_MODEL)),
                  _const_spec((D_MODEL, n))],
        out_specs=pl.BlockSpec((tm, n), lambda i: (i, 0)),
        out_shape=jax.ShapeDtypeStruct((t, n), _BF16),
        compiler_params=_params("parallel"),
        name="proj_natten",
    )(x2, g, w_in)


def _natten_pattern(step, n_steps):
    return jnp.where(step < 2, step, jnp.where(step >= n_steps - 2, step - n_steps + NAT_PATTERNS, 2))


def _natten_window_start(step, rows):
    first = jnp.clip(NAT_ROWS_PER_STEP * step - WIN_H // 2, 0, rows - WIN_H)
    return jnp.minimum(first, rows - NAT_WIN_ROWS)


def _natten_kernel(q_ref, k_ref, v_ref, tab_ref, o_ref, *, rows):
    n_steps = rows // NAT_ROWS_PER_STEP
    nq = NAT_ROWS_PER_STEP * GRID_W
    low_half = lax.broadcasted_iota(jnp.int32, (1, LANES), 1) < HEAD_DIM

    ones = jnp.ones((NAT_KEYS, LANES), _BF16)

    def one_step(step):
        key0 = pl.multiple_of(_natten_window_start(step, rows) * GRID_W, GRID_W)
        q0 = pl.multiple_of(step * nq, nq)
        kw = k_ref[0, pl.ds(key0, NAT_KEYS), :]
        vw = v_ref[0, pl.ds(key0, NAT_KEYS), :]
        q2 = q_ref[0, pl.ds(q0, nq), :]
        zero = jnp.zeros_like(q2)
        qm = jnp.concatenate([jnp.where(low_half, q2, zero), jnp.where(low_half, zero, q2)], axis=0)
        st = lax.dot_general(kw, qm, _NT, preferred_element_type=_F32)
        st = st + tab_ref[0, _natten_pattern(step, n_steps)]
        e = jnp.exp(st - jnp.max(st, axis=0, keepdims=True)).astype(_BF16)
        o2 = lax.dot_general(e, jnp.concatenate([vw, ones], axis=1), _TN,
                             preferred_element_type=_F32)
        o2 = o2[:, :LANES] / o2[:, LANES:]
        o_ref[0, pl.ds(q0, nq), :] = jnp.where(low_half, o2[:nq], o2[nq:]).astype(_BF16)

    def trip(i, carry):
        for u in range(NAT_UNROLL):
            one_step(i * NAT_UNROLL + u)
        return carry

    assert n_steps % NAT_UNROLL == 0
    lax.fori_loop(0, n_steps // NAT_UNROLL, trip, 0)


def _natten_bias_kernel(picked_ref, expand_ref, o_ref):
    o_ref[0] = jnp.dot(picked_ref[0], expand_ref[...], preferred_element_type=_F32,
                       precision=lax.Precision.HIGHEST)


def _natten_bias_table(rpb, rows):
    n_steps = rows // NAT_ROWS_PER_STEP
    assert rows % NAT_ROWS_PER_STEP == 0 and rows >= NAT_WIN_ROWS and n_steps >= NAT_PATTERNS
    steps = np.array([0, 1, 2, n_steps - 2, n_steps - 1])
    first = np.clip(NAT_ROWS_PER_STEP * steps - WIN_H // 2, 0, rows - WIN_H)
    win0 = np.minimum(first, rows - NAT_WIN_ROWS)
    i = np.arange(NAT_WIN_ROWS)[None, :, None, None, None]
    kc = np.arange(GRID_W)[None, None, :, None, None]
    j = np.arange(NAT_ROWS_PER_STEP)[None, None, None, :, None]
    qc = np.arange(GRID_W)[None, None, None, None, :]
    r = NAT_ROWS_PER_STEP * steps[:, None, None, None, None] + j
    key_row = win0[:, None, None, None, None] + i
    start = np.clip(r - WIN_H // 2, 0, rows - WIN_H)
    row_ok = (key_row >= start) & (key_row < start + WIN_H)
    col_start = np.clip(qc - WIN_W // 2, 0, GRID_W - WIN_W)
    col_ok = (kc >= col_start) & (kc < col_start + WIN_W)
    rel_row = np.clip(key_row - r + (WIN_H - 1), 0, 2 * WIN_H - 2)
    rel_col = np.clip(kc - qc + (WIN_W - 1), 0, 2 * WIN_W - 2)
    shape = (NAT_PATTERNS, NAT_WIN_ROWS, GRID_W, NAT_ROWS_PER_STEP, GRID_W)
    ok = np.broadcast_to(row_ok & col_ok, shape)
    n_heads, n_rel_col = rpb.shape[0], rpb.shape[2]
    row_sel = rel_row.reshape(-1)
    n_sel = -(-row_sel.size // SUBLANES) * SUBLANES
    picked = jnp.zeros((n_heads, n_sel, LANES), _F32)
    picked = picked.at[:, :row_sel.size, :n_rel_col].set(rpb.astype(_F32)[:, row_sel, :])
    expand = np.zeros((LANES, GRID_W * GRID_W), np.float32)
    expand[rel_col.reshape(-1), np.arange(GRID_W * GRID_W)] = 1.0
    tab = pl.pallas_call(
        _natten_bias_kernel,
        grid=(n_heads,),
        in_specs=[pl.BlockSpec((1, n_sel, LANES), lambda h: (h, 0, 0)),
                  _const_spec(expand.shape)],
        out_specs=pl.BlockSpec((1, n_sel, GRID_W * GRID_W), lambda h: (h, 0, 0)),
        out_shape=jax.ShapeDtypeStruct((n_heads, n_sel, GRID_W * GRID_W), _F32),
        compiler_params=_params("parallel"),
        name="natten_bias",
    )(picked, jnp.asarray(expand))
    nq = NAT_ROWS_PER_STEP * GRID_W
    tab = tab[:, :row_sel.size].reshape(n_heads, NAT_PATTERNS, NAT_WIN_ROWS, NAT_ROWS_PER_STEP, GRID_W, GRID_W)
    tab = tab.transpose(0, 1, 2, 4, 3, 5)
    tab = jnp.where(ok[None], tab, NEG_INF).reshape(N_PAIR, 2, NAT_PATTERNS, NAT_KEYS, nq)
    return tab.transpose(0, 2, 3, 1, 4).reshape(N_PAIR, NAT_PATTERNS, NAT_KEYS, 2 * nq)


def _natten(proj3, table):
    b, s, _ = proj3.shape
    rows = s // GRID_W
    blk = (1, s, LANES)
    return pl.pallas_call(
        functools.partial(_natten_kernel, rows=rows),
        grid=(N_PAIR, b),
        in_specs=[pl.BlockSpec(blk, lambda p, i: (i, 0, p)),
                  pl.BlockSpec(blk, lambda p, i: (i, 0, N_PAIR + p)),
                  pl.BlockSpec(blk, lambda p, i: (i, 0, 2 * N_PAIR + p)),
                  pl.BlockSpec((1,) + table.shape[1:], lambda p, i: (p, 0, 0, 0))],
        out_specs=pl.BlockSpec(blk, lambda p, i: (i, 0, p)),
        out_shape=jax.ShapeDtypeStruct((b, s, TOK_WIDTH), _BF16),
        compiler_params=_params("parallel", "parallel"),
        name="natten",
    )(proj3, proj3, proj3, table)


def _mixer_b_out_kernel(x_ref, tok_ref, qx_ref, k_ref, v_ref, wout_ref, g_ref, wr_ref,
                        o_ref, h_ref, idx_ref, gate_ref):
    x = _xattn_outproj(x_ref[0], tok_ref[0], qx_ref[0], k_ref[0], v_ref[0], wout_ref)
    o_ref[0] = x
    h = _rms(x, g_ref[...])
    h_ref[...] = _pack_halves(h)
    idx_ref[...], gate_ref[...] = _route(h, wr_ref)


def _mixer_b_out(x, tok, proj3, k, v, w_out, g_ffn, w_router2, *, tm):
    b, s, _ = x.shape
    t = b * s
    nj = s // tm
    qx_block = 3 * TOK_WIDTH // XA_WIDTH
    return pl.pallas_call(
        _mixer_b_out_kernel,
        grid=(b, nj),
        in_specs=[pl.BlockSpec((1, tm, D_MODEL), lambda i, j: (i, j, 0)),
                  pl.BlockSpec((1, tm, TOK_WIDTH), lambda i, j: (i, j, 0)),
                  pl.BlockSpec((1, tm, XA_WIDTH), lambda i, j: (i, j, qx_block)),
                  pl.BlockSpec((1, N_MEM, XA_WIDTH), lambda i, j: (i, 0, 0)),
                  pl.BlockSpec((1, N_MEM, XA_WIDTH), lambda i, j: (i, 0, 0)),
                  _const_spec((D_MODEL, D_MODEL)),
                  _const_spec((1, D_MODEL)),
                  _const_spec((D_MODEL, 2 * LANES))],
        out_specs=[pl.BlockSpec((1, tm, D_MODEL), lambda i, j: (i, j, 0)),
                   pl.BlockSpec((tm, D_MODEL // 2), lambda i, j: (i * nj + j, 0)),
                   pl.BlockSpec((SUBLANES, tm), lambda i, j: (0, i * nj + j)),
                   pl.BlockSpec((SUBLANES, tm), lambda i, j: (0, i * nj + j))],
        out_shape=[jax.ShapeDtypeStruct(x.shape, _F32),
                   jax.ShapeDtypeStruct((t, D_MODEL // 2), jnp.uint32),
                   jax.ShapeDtypeStruct((SUBLANES, t), jnp.int32),
                   jax.ShapeDtypeStruct((SUBLANES, t), _F32)],
        compiler_params=_params("parallel", "parallel"),
        name="mixer_natten_out",
    )(x, tok, proj3, k, v, w_out, g_ffn, w_router2)


def _pack_halves(h):
    half = D_MODEL // 2
    hi = pltpu.bitcast(h[:, :half].astype(_BF16).astype(_F32), jnp.uint32)
    lo = pltpu.bitcast(h[:, half:].astype(_BF16).astype(_F32), jnp.uint32)
    return hi | (lo >> 16)


def _unpack_halves(w):
    hi = pltpu.bitcast(w & jnp.uint32(0xFFFF0000), _F32).astype(_BF16)
    lo = pltpu.bitcast(w << 16, _F32).astype(_BF16)
    return jnp.concatenate([hi, lo], axis=1)


def _route(h, wr_ref):
    h_head = h.astype(_BF16)
    h_rest = (h - h_head.astype(_F32)).astype(_BF16)
    both = _dot(h_head, wr_ref[...])
    logits = both[:, :LANES] + both[:, LANES:] + _dot(h_rest, wr_ref[:, :LANES])
    assert N_EXPERTS == SUBLANES
    lt = jnp.concatenate([jnp.transpose(logits[c * LANES:(c + 1) * LANES])[:SUBLANES]
                          for c in range(h.shape[0] // LANES)], axis=1)
    expert = lax.broadcasted_iota(jnp.int32, lt.shape, 0)
    v1 = jnp.max(lt, axis=0, keepdims=True)
    i1 = jnp.min(jnp.where(lt == v1, expert, N_EXPERTS), axis=0, keepdims=True)
    lt2 = jnp.where(expert == i1, -jnp.inf, lt)
    v2 = jnp.max(lt2, axis=0, keepdims=True)
    i2 = jnp.min(jnp.where(lt2 == v2, expert, N_EXPERTS), axis=0, keepdims=True)
    e2 = jnp.exp(v2 - v1)
    g1 = 1.0 / (1.0 + e2)
    g2 = e2 / (1.0 + e2)
    idx = jnp.where(expert == 0, i1, jnp.where(expert == 1, i2, 0))
    gate = jnp.where(expert == 0, g1, jnp.where(expert == 1, g2, 0.0))
    return idx, gate


def _gather_rows(table, idx):
    n, d = idx.shape[0], table.shape[1]
    workers = SC_CORES * SC_SUBCORES
    per_worker = n // workers
    pairs = per_worker // (2 * SC_WINDOW)
    assert n == workers * pairs * 2 * SC_WINDOW, (n, workers, SC_WINDOW)
    mesh = plsc.VectorSubcoreMesh(core_axis_name="core", subcore_axis_name="subcore",
                                  num_cores=SC_CORES, num_subcores=SC_SUBCORES)

    def gather_kernel(table_hbm, idx_hbm, out_hbm, idx_a, idx_b, rows_a, rows_b, sem_a, sem_b):
        worker = lax.axis_index("subcore") * SC_CORES + lax.axis_index("core")
        base = worker * per_worker

        @pl.loop(0, pairs)
        def _(p):
            r0 = base + p * (2 * SC_WINDOW)
            r1 = r0 + SC_WINDOW
            pltpu.sync_copy(idx_hbm.at[pl.ds(r0, SC_WINDOW)], idx_a)
            gather_a = pltpu.async_copy(table_hbm.at[idx_a], rows_a, sem_a)
            pltpu.sync_copy(idx_hbm.at[pl.ds(r1, SC_WINDOW)], idx_b)
            gather_b = pltpu.async_copy(table_hbm.at[idx_b], rows_b, sem_b)
            gather_a.wait()
            pltpu.sync_copy(rows_a, out_hbm.at[pl.ds(r0, SC_WINDOW)])
            gather_b.wait()
            pltpu.sync_copy(rows_b, out_hbm.at[pl.ds(r1, SC_WINDOW)])

    idx_buf = pltpu.VMEM((SC_WINDOW,), jnp.int32)
    row_buf = pltpu.VMEM((SC_WINDOW, d), table.dtype)
    return pl.kernel(gather_kernel, out_type=jax.ShapeDtypeStruct((n, d), table.dtype), mesh=mesh,
                     scratch_types=[idx_buf, idx_buf, row_buf, row_buf,
                                    pltpu.SemaphoreType.DMA, pltpu.SemaphoreType.DMA],
                     name="moe_row_gather")(table, idx)


def _expert_kernel(meta_ref, xs_ref, wgu_ref, wd_ref, ys_ref, *, n_blocks):
    @pl.when(pl.program_id(0) < meta_ref[n_blocks])
    def _():
        h = _unpack_halves(xs_ref[...])
        acc = jnp.zeros((MOE_ROWS, D_MODEL), _F32)
        for c in range(D_FF_EXPERT // FF_CHUNK_EXPERT):
            lo = c * FF_CHUNK_EXPERT
            gate = _dot(h, wgu_ref[0, :, lo:lo + FF_CHUNK_EXPERT])
            up = _dot(h, wgu_ref[0, :, D_FF_EXPERT + lo:D_FF_EXPERT + lo + FF_CHUNK_EXPERT])
            a = (_silu(gate) * up).astype(_BF16)
            acc = acc + _dot(a, wd_ref[0, lo:lo + FF_CHUNK_EXPERT, :])
        ys_ref[...] = _pack_halves(acc)


def _experts(meta, xs, w_gu, w_down):
    n_rows, half = xs.shape
    n_blocks = n_rows // MOE_ROWS
    grid_spec = pltpu.PrefetchScalarGridSpec(
        num_scalar_prefetch=1,
        grid=(n_blocks,),
        in_specs=[pl.BlockSpec((MOE_ROWS, half), lambda i, m: (i, 0)),
                  pl.BlockSpec((1, D_MODEL, 2 * D_FF_EXPERT), lambda i, m: (m[i], 0, 0),
                               pipeline_mode=pl.Buffered(1)),
                  pl.BlockSpec((1, D_FF_EXPERT, D_MODEL), lambda i, m: (m[i], 0, 0),
                               pipeline_mode=pl.Buffered(1))],
        out_specs=pl.BlockSpec((MOE_ROWS, half), lambda i, m: (i, 0)),
    )
    return pl.pallas_call(
        functools.partial(_expert_kernel, n_blocks=n_blocks),
        grid_spec=grid_spec,
        out_shape=jax.ShapeDtypeStruct((n_rows, half), jnp.uint32),
        compiler_params=_params("arbitrary"),
        name="moe_experts",
    )(meta, xs, w_gu, w_down)


def _combine_kernel(x_ref, y0_ref, y1_ref, gate_ref, g_ref, o_ref, *, tc):
    gates = gate_ref[...]
    pad = jnp.zeros((LANES - SUBLANES, LANES), _F32)
    cols = [jnp.transpose(jnp.concatenate([gates[:, c * LANES:(c + 1) * LANES], pad], axis=0))
            for c in range(tc // LANES)]
    gcol = jnp.concatenate(cols, axis=0)
    y0 = _unpack_halves(y0_ref[...]).astype(_F32)
    y1 = _unpack_halves(y1_ref[...]).astype(_F32)
    f = y0 * gcol[:, 0:1] + y1 * gcol[:, 1:2]
    o_ref[...] = _rms(x_ref[...] + f, g_ref[...])


def _combine(x2, y0, y1, gates, g_final, *, tc):
    t = x2.shape[0]
    half = y0.shape[1]
    return pl.pallas_call(
        functools.partial(_combine_kernel, tc=tc),
        grid=(t // tc,),
        in_specs=[pl.BlockSpec((tc, D_MODEL), lambda i: (i, 0)),
                  pl.BlockSpec((tc, half), lambda i: (i, 0)),
                  pl.BlockSpec((tc, half), lambda i: (i, 0)),
                  pl.BlockSpec((SUBLANES, tc), lambda i: (0, i)),
                  _const_spec((1, D_MODEL))],
        out_specs=pl.BlockSpec((tc, D_MODEL), lambda i: (i, 0)),
        out_shape=jax.ShapeDtypeStruct(x2.shape, _F32),
        compiler_params=_params("parallel"),
        name="moe_combine",
    )(x2, y0, y1, gates, g_final)


def _routing_offsets(expert_idx, n_blocks):
    t = expert_idx.shape[1]
    n = 2 * t
    flat_e = expert_idx[:2].reshape(n)
    experts = jnp.arange(N_EXPERTS, dtype=jnp.int32)
    onehot = (flat_e[:, None] == experts[None, :]).astype(jnp.int32)
    running = jnp.cumsum(onehot, axis=0)
    rank = jnp.sum((running - onehot) * onehot, axis=1)
    counts = running[-1]
    padded = (counts + MOE_ROWS - 1) // MOE_ROWS * MOE_ROWS
    pad_end = jnp.cumsum(padded)
    pad_start = pad_end - padded
    start = jnp.cumsum(counts) - counts
    dest = (jnp.sum(onehot * pad_start[None, :], axis=1) + rank).astype(jnp.int32)

    block_row = jnp.arange(n_blocks, dtype=jnp.int32) * MOE_ROWS
    block_expert = jnp.minimum(jnp.sum((block_row[:, None] >= pad_end[None, :]).astype(jnp.int32), axis=1),
                               N_EXPERTS - 1)
    meta = jnp.concatenate([block_expert, pad_end[-1:] // MOE_ROWS]).astype(jnp.int32)

    order = jnp.argsort(flat_e, stable=True).astype(jnp.int32)
    row_onehot = jnp.repeat((block_expert[:, None] == experts[None, :]).astype(jnp.int32), MOE_ROWS, axis=0)
    off = jnp.arange(n_blocks * MOE_ROWS, dtype=jnp.int32) - jnp.sum(row_onehot * pad_start[None, :], axis=1)
    valid = off < jnp.sum(row_onehot * counts[None, :], axis=1)
    pos = jnp.clip(jnp.sum(row_onehot * start[None, :], axis=1) + off, 0, n - 1)
    src_tok = jnp.where(valid, order[pos] % t, 0).astype(jnp.int32)
    return dest, src_tok, meta


def _trunk(x, mem, w, w_experts=None):
    b, s, d = x.shape
    t = b * s

    k0, v0 = _mem_kv(mem, w['norm_mem'][0], w['w_mem_kv'][0])
    x = _mixer_a(x, k0, v0, w['norm_mix'][0], w['w_in_a'], w['g_v_a'], w['w_s_a'], w['b_s_a'],
                 w['w_out'][0], tm=ROW_TILE)
    side = () if w_experts is not None else (w['w_exp_gu'], w['w_exp_down'])
    x2, cast = _ffn(x.reshape(t, d), w['norm_ffn'][0], w['w_ffn_gu'], w['w_ffn_down'], side, tm=ROW_TILE)
    w_exp_gu, w_exp_down = w_experts if w_experts is not None else cast

    k1, v1 = _mem_kv(mem, w['norm_mem'][1], w['w_mem_kv'][1])
    proj = _proj_b(x2, w['norm_mix'][1], w['w_in_b'], tm=ROW_TILE).reshape(b, s, -1)
    tok = _natten(proj, _natten_bias_table(w['rpb_b'], s // GRID_W))
    x, h_packed, expert_idx, gates = _mixer_b_out(x2.reshape(b, s, d), tok, proj, k1, v1, w['w_out'][1],
                                                  w['norm_ffn'][1], w['w_router'], tm=ROW_TILE)
    x2 = x.reshape(t, d)

    n_blocks = 2 * t // MOE_ROWS + N_EXPERTS
    dest, src_tok, meta = _routing_offsets(expert_idx, n_blocks)
    xs = _gather_rows(h_packed, src_tok)
    ys = _experts(meta, xs, w_exp_gu, w_exp_down)
    y0 = _gather_rows(ys, dest[:t])
    y1 = _gather_rows(ys, dest[t:])
    y = _combine(x2, y0, y1, gates, w['norm_final'], tc=ROW_TILE)
    return y.reshape(b, s, d), (w_exp_gu, w_exp_down)


def _cast_kernel(w_ref, o_ref):
    o_ref[...] = w_ref[...].astype(o_ref.dtype)


def _to_bf16(w, rows_per_step):
    n, r, c = w.shape
    assert r % rows_per_step == 0
    block = (1, rows_per_step, c)
    return pl.pallas_call(
        _cast_kernel,
        grid=(n, r // rows_per_step),
        in_specs=[pl.BlockSpec(block, lambda i, j: (i, j, 0))],
        out_specs=pl.BlockSpec(block, lambda i, j: (i, j, 0)),
        out_shape=jax.ShapeDtypeStruct(w.shape, _BF16),
        compiler_params=_params("parallel", "parallel"),
        name="weight_cast",
    )(w)


def _split_router_weight(w):
    head = w.astype(_BF16)
    rest = (w - head.astype(_F32)).astype(_BF16)
    pad = ((0, 0), (0, LANES - N_EXPERTS))
    return jnp.concatenate([jnp.pad(head, pad), jnp.pad(rest, pad)], axis=1)


def _prepare_weights(norm_mix, norm_mem, norm_ffn, norm_final, w_in_a, g_v_a, w_s_a, b_s_a, w_in_b, rpb_b,
                     w_mem_kv, w_out, w_ffn_gu, w_ffn_down, w_router, w_exp_gu, w_exp_down):
    assert w_in_a.shape[0] == 1 and w_in_b.shape[0] == 1 and norm_mix.shape[0] == 2
    row = lambda a: a.astype(_F32)[:, None, :]
    return {
        'norm_mix': row(norm_mix), 'norm_mem': row(norm_mem), 'norm_ffn': row(norm_ffn),
        'norm_final': norm_final.astype(_F32)[None, :],
        'w_in_a': w_in_a[0].astype(_BF16),
        'g_v_a': g_v_a.astype(_F32),
        'w_s_a': jnp.concatenate([w_s_a[0, 0::2], w_s_a[0, 1::2]], axis=2).astype(_BF16),
        'b_s_a': jnp.repeat(b_s_a[0].astype(_F32).T, HEAD_DIM, axis=1),
        'w_in_b': w_in_b[0].astype(_BF16),
        'rpb_b': rpb_b[0],
        'w_mem_kv': w_mem_kv.astype(_BF16),
        'w_out': w_out.astype(_BF16),
        'w_ffn_gu': _to_bf16(w_ffn_gu, 256)[0],
        'w_ffn_down': _to_bf16(w_ffn_down, 704)[0],
        'w_router': _split_router_weight(w_router[0].astype(_F32)),
        'w_exp_gu': w_exp_gu[0].astype(_F32),
        'w_exp_down': w_exp_down[0].astype(_F32),
    }


def kernel(x_prompt, x_sample, mem_prompt, mem_sample, norm_mix, norm_mem, norm_ffn, norm_final, w_in_a, g_v_a, w_s_a, b_s_a, w_in_b, rpb_b, w_mem_kv, w_out, w_ffn_gu, w_ffn_down, w_router, w_exp_gu, w_exp_down):
    w = _prepare_weights(norm_mix, norm_mem, norm_ffn, norm_final, w_in_a, g_v_a, w_s_a, b_s_a, w_in_b, rpb_b,
                         w_mem_kv, w_out, w_ffn_gu, w_ffn_down, w_router, w_exp_gu, w_exp_down)
    y_prompt, w_experts = _trunk(x_prompt, mem_prompt, w)
    y_sample, _ = _trunk(x_sample, mem_sample, w, w_experts)
    return (y_prompt, y_sample)
```

```python
import functools

import numpy as np
import jax
import jax.numpy as jnp
from jax import lax
from jax.experimental import pallas as pl
from jax.experimental.pallas import tpu as pltpu
from jax.experimental.pallas import tpu_sc as plsc

D_MODEL = 1024
HEAD_DIM = 64
XA_HEADS = 4
XA_WIDTH = XA_HEADS * HEAD_DIM
TOK_WIDTH = D_MODEL - XA_WIDTH
CHUNK = 128
N_PAIR = TOK_WIDTH // (2 * HEAD_DIM)
GRID_W = 64
WIN_H = 8
WIN_W = 16
N_MEM = 256
D_FF = 2816
N_EXPERTS = 8
D_FF_EXPERT = 3584
RMS_EPS = 1e-6
NEG_INF = -1e30
ATTN_SCALE = HEAD_DIM ** -0.5

LANES = 128
SUBLANES = 8
VMEM_LIMIT = 56 * 1024 * 1024

NAT_ROWS_PER_STEP = 2
NAT_WIN_ROWS = WIN_H + NAT_ROWS_PER_STEP - 1
NAT_KEYS = NAT_WIN_ROWS * GRID_W
NAT_PATTERNS = 5
NAT_UNROLL = 4

MOE_ROWS = 512
SC_CORES = 2
SC_SUBCORES = 16
SC_WINDOW = 64
MXU_WIDTH = 256
FF_SPLIT_DENSE = 6 * MXU_WIDTH
ROW_TILE = 512
FF_CHUNK_EXPERT = 512

_BF16 = jnp.bfloat16
_F32 = jnp.float32
_NT = (((1,), (1,)), ((), ()))
_TN = (((0,), (0,)), ((), ()))


def _params(*semantics):
    return pltpu.CompilerParams(dimension_semantics=semantics, vmem_limit_bytes=VMEM_LIMIT)


def _const_spec(shape):
    zeros = (0,) * len(shape)
    return pl.BlockSpec(shape, lambda *_: zeros, pipeline_mode=pl.Buffered(1))


def _rms(x, g):
    ms = jnp.mean(x * x, axis=-1, keepdims=True)
    return x * lax.rsqrt(ms + RMS_EPS) * g


def _dot(a, b):
    return jnp.dot(a, b, preferred_element_type=_F32)


def _gelu(x):
    return 0.5 * x * (1.0 + lax.erf(x * np.float32(1.0 / np.sqrt(2.0))))


def _silu(x):
    return x / (1.0 + jnp.exp(-x))


def _softmax_rows(s):
    m = jnp.max(s, axis=-1, keepdims=True)
    e = jnp.exp(s - m)
    return e / jnp.sum(e, axis=-1, keepdims=True)


def _xattn_outproj(x, tok, qx, k, v, wout_ref):
    lane = lax.broadcasted_iota(jnp.int32, (1, XA_WIDTH), 1)
    qx = qx.astype(_F32)
    xa = jnp.zeros(qx.shape, _F32)
    for h in range(XA_HEADS):
        in_head = (lane >= h * HEAD_DIM) & (lane < (h + 1) * HEAD_DIM)
        qh = jnp.where(in_head, qx, 0.0).astype(_BF16)
        s = lax.dot_general(qh, k, _NT, preferred_element_type=_F32)
        p = _softmax_rows(s).astype(_BF16)
        xa = jnp.where(in_head, _dot(p, v), xa)
    y = _dot(tok, wout_ref[:TOK_WIDTH, :]) + _dot(xa.astype(_BF16), wout_ref[TOK_WIDTH:, :])
    return x + y


def _mem_kv_kernel(mem_ref, g_ref, w_ref, k_ref, v_ref):
    m = _rms(mem_ref[0], g_ref[...]).astype(_BF16)
    kv = _dot(m, w_ref[...])
    k_ref[0] = kv[:, :XA_WIDTH].astype(_BF16)
    v_ref[0] = kv[:, XA_WIDTH:].astype(_BF16)


def _mem_kv(mem, g, w_kv):
    b = mem.shape[0]
    out = jax.ShapeDtypeStruct((b, N_MEM, XA_WIDTH), _BF16)
    return pl.pallas_call(
        _mem_kv_kernel,
        grid=(b,),
        in_specs=[pl.BlockSpec((1, N_MEM, D_MODEL), lambda i: (i, 0, 0)),
                  _const_spec((1, D_MODEL)),
                  _const_spec((D_MODEL, 2 * XA_WIDTH))],
        out_specs=[pl.BlockSpec((1, N_MEM, XA_WIDTH), lambda i: (i, 0, 0))] * 2,
        out_shape=[out, out],
        compiler_params=_params("parallel"),
        name="mem_kv",
    )(mem, g, w_kv)


def _mixer_a_kernel(x_ref, k_ref, v_ref, g_ref, win_ref, gv_ref, ws_ref, bs_ref, wout_ref, o_ref, *, tm):
    assert tm % (2 * CHUNK) == 0
    x = x_ref[0]
    h = _rms(x, g_ref[...]).astype(_BF16)
    proj = _dot(h, win_ref[...])
    z = _gelu(proj[:, :2 * TOK_WIDTH])
    u = z[:, :TOK_WIDTH]
    vn = _rms(z[:, TOK_WIDTH:], gv_ref[...]).astype(_BF16)
    low_half = lax.broadcasted_iota(jnp.int32, (1, LANES), 1) < HEAD_DIM
    bias = bs_ref[...]
    zero = jnp.zeros((CHUNK, LANES), _BF16)
    toks = []
    for c in range(0, tm // CHUNK, 2):
        va, vb = vn[c * CHUNK:(c + 1) * CHUNK], vn[(c + 1) * CHUNK:(c + 2) * CHUNK]
        cols_a, cols_b = [], []
        for j in range(N_PAIR):
            pa, pb = va[:, j * LANES:(j + 1) * LANES], vb[:, j * LANES:(j + 1) * LANES]
            even = jnp.concatenate([jnp.where(low_half, pa, zero), jnp.where(low_half, pb, zero)], axis=1)
            odd = jnp.concatenate([jnp.where(low_half, zero, pa), jnp.where(low_half, zero, pb)], axis=1)
            r = _dot(ws_ref[j], jnp.concatenate([even, odd], axis=0))
            cols_a.append(r[:, :LANES])
            cols_b.append(r[:, LANES:])
        for side, cols in enumerate((cols_a, cols_b)):
            sv = jnp.concatenate(cols, axis=1) + bias
            toks.append(u[(c + side) * CHUNK:(c + side + 1) * CHUNK] * sv)
    tok = jnp.concatenate(toks, axis=0).astype(_BF16)
    qx = proj[:, 2 * TOK_WIDTH:] * ATTN_SCALE
    o_ref[0] = _xattn_outproj(x, tok, qx, k_ref[0], v_ref[0], wout_ref)


def _mixer_a(x, k, v, g, w_in, g_v, w_s2, b_s2, w_out, *, tm):
    b, s, _ = x.shape
    return pl.pallas_call(
        functools.partial(_mixer_a_kernel, tm=tm),
        grid=(b, s // tm),
        in_specs=[pl.BlockSpec((1, tm, D_MODEL), lambda i, j: (i, j, 0)),
                  pl.BlockSpec((1, N_MEM, XA_WIDTH), lambda i, j: (i, 0, 0)),
                  pl.BlockSpec((1, N_MEM, XA_WIDTH), lambda i, j: (i, 0, 0)),
                  _const_spec((1, D_MODEL)),
                  _const_spec((D_MODEL, 2 * TOK_WIDTH + XA_WIDTH)),
                  _const_spec((1, TOK_WIDTH)),
                  _const_spec((N_PAIR, CHUNK, 2 * CHUNK)),
                  _const_spec((CHUNK, TOK_WIDTH)),
                  _const_spec((D_MODEL, D_MODEL))],
        out_specs=pl.BlockSpec((1, tm, D_MODEL), lambda i, j: (i, j, 0)),
        out_shape=jax.ShapeDtypeStruct(x.shape, _F32),
        compiler_params=_params("parallel", "parallel"),
        name="mixer_gmlp",
    )(x, k, v, g, w_in, g_v, w_s2, b_s2, w_out)


def _ffn_kernel(x_ref, g_ref, wgu_ref, wd_ref, *refs):
    n_side = (len(refs) - 1) // 2
    o_ref = refs[n_side]
    x = x_ref[...]
    h = _rms(x, g_ref[...]).astype(_BF16)
    acc = x
    for lo, hi in ((0, FF_SPLIT_DENSE), (FF_SPLIT_DENSE, D_FF)):
        gate = _dot(h, wgu_ref[:, lo:hi])
        up = _dot(h, wgu_ref[:, D_FF + lo:D_FF + hi])
        a = (_silu(gate) * up).astype(_BF16)
        acc = acc + _dot(a, wd_ref[lo:hi, :])
    o_ref[...] = acc
    for src, dst in zip(refs[:n_side], refs[n_side + 1:]):
        dst[...] = src[...].astype(dst.dtype)


def _ffn(x2, g, w_gu, w_down, side_casts=(), *, tm):
    t = x2.shape[0]
    steps = t // tm
    flat = [w.reshape(-1, w.shape[-1]) for w in side_casts]
    slabs = [(w.shape[0] // steps, w.shape[1]) for w in flat]
    assert all(w.shape[0] == s[0] * steps and s[0] % (2 * SUBLANES) == 0 for w, s in zip(flat, slabs))
    side_specs = [pl.BlockSpec(s, lambda i: (i, 0)) for s in slabs]
    outs = pl.pallas_call(
        _ffn_kernel,
        grid=(steps,),
        in_specs=[pl.BlockSpec((tm, D_MODEL), lambda i: (i, 0)),
                  _const_spec((1, D_MODEL)),
                  _const_spec((D_MODEL, 2 * D_FF)),
                  _const_spec((D_FF, D_MODEL))] + side_specs,
        out_specs=[pl.BlockSpec((tm, D_MODEL), lambda i: (i, 0))] + side_specs,
        out_shape=[jax.ShapeDtypeStruct(x2.shape, _F32)] + [jax.ShapeDtypeStruct(w.shape, _BF16) for w in flat],
        compiler_params=_params("parallel"),
        name="ffn_dense",
    )(x2, g, w_gu, w_down, *flat)
    return outs[0], [o.reshape(w.shape) for o, w in zip(outs[1:], side_casts)]


def _proj_b_kernel(x_ref, g_ref, w_ref, o_ref):
    h = _rms(x_ref[...], g_ref[...]).astype(_BF16)
    proj = _dot(h, w_ref[...])
    col = lax.broadcasted_iota(jnp.int32, (1, proj.shape[1]), 1)
    is_query = (col < TOK_WIDTH) | (col >= 3 * TOK_WIDTH)
    o_ref[...] = (proj * jnp.where(is_query, ATTN_SCALE, 1.0)).astype(_BF16)


def _proj_b(x2, g, w_in, *, tm):
    t = x2.shape[0]
    n = w_in.shape[1]
    return pl.pallas_call(
        _proj_b_kernel,
        grid=(t // tm,),
        in_specs=[pl.BlockSpec((tm, D_MODEL), lambda i: (i, 0)),
                  _const_spec((1, D_MODEL)),
                  _const_spec((D_MODEL, n))],
        out_specs=pl.BlockSpec((tm, n), lambda i: (i, 0)),
        out_shape=jax.ShapeDtypeStruct((t, n), _BF16),
        compiler_params=_params("parallel"),
        name="proj_natten",
    )(x2, g, w_in)


def _natten_pattern(step, n_steps):
    return jnp.where(step < 2, step, jnp.where(step >= n_steps - 2, step - n_steps + NAT_PATTERNS, 2))


def _natten_window_start(step, rows):
    first = jnp.clip(NAT_ROWS_PER_STEP * step - WIN_H // 2, 0, rows - WIN_H)
    return jnp.minimum(first, rows - NAT_WIN_ROWS)


def _natten_kernel(q_ref, k_ref, v_ref, tab_ref, *rest, rows):
    o_ref = rest[-1]
    n_steps = rows // NAT_ROWS_PER_STEP
    nq = NAT_ROWS_PER_STEP * GRID_W
    low_half = lax.broadcasted_iota(jnp.int32, (1, LANES), 1) < HEAD_DIM

    ones = jnp.ones((NAT_KEYS, LANES), _BF16)

    def one_step(step):
        key0 = pl.multiple_of(_natten_window_start(step, rows) * GRID_W, GRID_W)
        q0 = pl.multiple_of(step * nq, nq)
        kw = k_ref[0, pl.ds(key0, NAT_KEYS), :]
        vw = v_ref[0, pl.ds(key0, NAT_KEYS), :]
        q2 = q_ref[0, pl.ds(q0, nq), :]
        zero = jnp.zeros_like(q2)
        qm = jnp.concatenate([jnp.where(low_half, q2, zero), jnp.where(low_half, zero, q2)], axis=0)
        st = lax.dot_general(kw, qm, _NT, preferred_element_type=_F32)
        st = st + tab_ref[0, _natten_pattern(step, n_steps)]
        e = jnp.exp(st - jnp.max(st, axis=0, keepdims=True)).astype(_BF16)
        o2 = lax.dot_general(e, jnp.concatenate([vw, ones], axis=1), _TN,
                             preferred_element_type=_F32)
        o2 = o2[:, :LANES] / o2[:, LANES:]
        o_ref[0, pl.ds(q0, nq), :] = jnp.where(low_half, o2[:nq], o2[nq:]).astype(_BF16)

    def trip(i, carry):
        for u in range(NAT_UNROLL):
            one_step(i * NAT_UNROLL + u)
        return carry

    assert n_steps % NAT_UNROLL == 0
    lax.fori_loop(0, n_steps // NAT_UNROLL, trip, 0)


def _natten_bias_kernel(picked_ref, expand_ref, o_ref):
    o_ref[0] = jnp.dot(picked_ref[0], expand_ref[...], preferred_element_type=_F32,
                       precision=lax.Precision.HIGHEST)


def _natten_bias_table(rpb, rows):
    n_steps = rows // NAT_ROWS_PER_STEP
    assert rows % NAT_ROWS_PER_STEP == 0 and rows >= NAT_WIN_ROWS and n_steps >= NAT_PATTERNS
    steps = np.array([0, 1, 2, n_steps - 2, n_steps - 1])
    first = np.clip(NAT_ROWS_PER_STEP * steps - WIN_H // 2, 0, rows - WIN_H)
    win0 = np.minimum(first, rows - NAT_WIN_ROWS)
    i = np.arange(NAT_WIN_ROWS)[None, :, None, None, None]
    kc = np.arange(GRID_W)[None, None, :, None, None]
    j = np.arange(NAT_ROWS_PER_STEP)[None, None, None, :, None]
    qc = np.arange(GRID_W)[None, None, None, None, :]
    r = NAT_ROWS_PER_STEP * steps[:, None, None, None, None] + j
    key_row = win0[:, None, None, None, None] + i
    start = np.clip(r - WIN_H // 2, 0, rows - WIN_H)
    row_ok = (key_row >= start) & (key_row < start + WIN_H)
    col_start = np.clip(qc - WIN_W // 2, 0, GRID_W - WIN_W)
    col_ok = (kc >= col_start) & (kc < col_start + WIN_W)
    rel_row = np.clip(key_row - r + (WIN_H - 1), 0, 2 * WIN_H - 2)
    rel_col = np.clip(kc - qc + (WIN_W - 1), 0, 2 * WIN_W - 2)
    shape = (NAT_PATTERNS, NAT_WIN_ROWS, GRID_W, NAT_ROWS_PER_STEP, GRID_W)
    ok = np.broadcast_to(row_ok & col_ok, shape)
    n_heads, n_rel_col = rpb.shape[0], rpb.shape[2]
    row_sel = rel_row.reshape(-1)
    n_sel = -(-row_sel.size // SUBLANES) * SUBLANES
    picked = jnp.zeros((n_heads, n_sel, LANES), _F32)
    picked = picked.at[:, :row_sel.size, :n_rel_col].set(rpb.astype(_F32)[:, row_sel, :])
    expand = np.zeros((LANES, GRID_W * GRID_W), np.float32)
    expand[rel_col.reshape(-1), np.arange(GRID_W * GRID_W)] = 1.0
    tab = pl.pallas_call(
        _natten_bias_kernel,
        grid=(n_heads,),
        in_specs=[pl.BlockSpec((1, n_sel, LANES), lambda h: (h, 0, 0)),
                  _const_spec(expand.shape)],
        out_specs=pl.BlockSpec((1, n_sel, GRID_W * GRID_W), lambda h: (h, 0, 0)),
        out_shape=jax.ShapeDtypeStruct((n_heads, n_sel, GRID_W * GRID_W), _F32),
        compiler_params=_params("parallel"),
        name="natten_bias",
    )(picked, jnp.asarray(expand))
    nq = NAT_ROWS_PER_STEP * GRID_W
    tab = tab[:, :row_sel.size].reshape(n_heads, NAT_PATTERNS, NAT_WIN_ROWS, NAT_ROWS_PER_STEP, GRID_W, GRID_W)
    tab = tab.transpose(0, 1, 2, 4, 3, 5)
    tab = jnp.where(ok[None], tab, NEG_INF).reshape(N_PAIR, 2, NAT_PATTERNS, NAT_KEYS, nq)
    return tab.transpose(0, 2, 3, 1, 4).reshape(N_PAIR, NAT_PATTERNS, NAT_KEYS, 2 * nq)


def _natten(proj3, table, run_after=None):
    b, s, _ = proj3.shape
    rows = s // GRID_W
    blk = (1, s, LANES)
    order_specs = [] if run_after is None else [pl.BlockSpec((SUBLANES, LANES), lambda p, i: (0, 0))]
    order_args = [] if run_after is None else [run_after]
    return pl.pallas_call(
        functools.partial(_natten_kernel, rows=rows),
        grid=(N_PAIR, b),
        in_specs=[pl.BlockSpec(blk, lambda p, i: (i, 0, p)),
                  pl.BlockSpec(blk, lambda p, i: (i, 0, N_PAIR + p)),
                  pl.BlockSpec(blk, lambda p, i: (i, 0, 2 * N_PAIR + p)),
                  pl.BlockSpec((1,) + table.shape[1:], lambda p, i: (p, 0, 0, 0))] + order_specs,
        out_specs=pl.BlockSpec(blk, lambda p, i: (i, 0, p)),
        out_shape=jax.ShapeDtypeStruct((b, s, TOK_WIDTH), _BF16),
        compiler_params=_params("parallel", "parallel"),
        name="natten",
    )(proj3, proj3, proj3, table, *order_args)


def _mixer_b_out_kernel(x_ref, tok_ref, qx_ref, k_ref, v_ref, wout_ref, g_ref, wr_ref,
                        o_ref, h_ref, idx_ref, gate_ref):
    x = _xattn_outproj(x_ref[0], tok_ref[0], qx_ref[0], k_ref[0], v_ref[0], wout_ref)
    o_ref[0] = x
    h = _rms(x, g_ref[...])
    h_ref[...] = _pack_halves(h)
    idx_ref[...], gate_ref[...] = _route(h, wr_ref)


def _mixer_b_out(x, tok, proj3, k, v, w_out, g_ffn, w_router2, *, tm):
    b, s, _ = x.shape
    t = b * s
    nj = s // tm
    qx_block = 3 * TOK_WIDTH // XA_WIDTH
    return pl.pallas_call(
        _mixer_b_out_kernel,
        grid=(b, nj),
        in_specs=[pl.BlockSpec((1, tm, D_MODEL), lambda i, j: (i, j, 0)),
                  pl.BlockSpec((1, tm, TOK_WIDTH), lambda i, j: (i, j, 0)),
                  pl.BlockSpec((1, tm, XA_WIDTH), lambda i, j: (i, j, qx_block)),
                  pl.BlockSpec((1, N_MEM, XA_WIDTH), lambda i, j: (i, 0, 0)),
                  pl.BlockSpec((1, N_MEM, XA_WIDTH), lambda i, j: (i, 0, 0)),
                  _const_spec((D_MODEL, D_MODEL)),
                  _const_spec((1, D_MODEL)),
                  _const_spec((D_MODEL, 2 * LANES))],
        out_specs=[pl.BlockSpec((1, tm, D_MODEL), lambda i, j: (i, j, 0)),
                   pl.BlockSpec((tm, D_MODEL // 2), lambda i, j: (i * nj + j, 0)),
                   pl.BlockSpec((SUBLANES, tm), lambda i, j: (0, i * nj + j)),
                   pl.BlockSpec((SUBLANES, tm), lambda i, j: (0, i * nj + j))],
        out_shape=[jax.ShapeDtypeStruct(x.shape, _F32),
                   jax.ShapeDtypeStruct((t, D_MODEL // 2), jnp.uint32),
                   jax.ShapeDtypeStruct((SUBLANES, t), jnp.int32),
                   jax.ShapeDtypeStruct((SUBLANES, t), _F32)],
        compiler_params=_params("parallel", "parallel"),
        name="mixer_natten_out",
    )(x, tok, proj3, k, v, w_out, g_ffn, w_router2)


def _pack_halves(h):
    half = D_MODEL // 2
    hi = pltpu.bitcast(h[:, :half].astype(_BF16).astype(_F32), jnp.uint32)
    lo = pltpu.bitcast(h[:, half:].astype(_BF16).astype(_F32), jnp.uint32)
    return hi | (lo >> 16)


def _unpack_halves(w):
    hi = pltpu.bitcast(w & jnp.uint32(0xFFFF0000), _F32).astype(_BF16)
    lo = pltpu.bitcast(w << 16, _F32).astype(_BF16)
    return jnp.concatenate([hi, lo], axis=1)


def _route(h, wr_ref):
    h_head = h.astype(_BF16)
    h_rest = (h - h_head.astype(_F32)).astype(_BF16)
    both = _dot(h_head, wr_ref[...])
    logits = both[:, :LANES] + both[:, LANES:] + _dot(h_rest, wr_ref[:, :LANES])
    assert N_EXPERTS == SUBLANES
    lt = jnp.concatenate([jnp.transpose(logits[c * LANES:(c + 1) * LANES])[:SUBLANES]
                          for c in range(h.shape[0] // LANES)], axis=1)
    expert = lax.broadcasted_iota(jnp.int32, lt.shape, 0)
    v1 = jnp.max(lt, axis=0, keepdims=True)
    i1 = jnp.min(jnp.where(lt == v1, expert, N_EXPERTS), axis=0, keepdims=True)
    lt2 = jnp.where(expert == i1, -jnp.inf, lt)
    v2 = jnp.max(lt2, axis=0, keepdims=True)
    i2 = jnp.min(jnp.where(lt2 == v2, expert, N_EXPERTS), axis=0, keepdims=True)
    e2 = jnp.exp(v2 - v1)
    g1 = 1.0 / (1.0 + e2)
    g2 = e2 / (1.0 + e2)
    idx = jnp.where(expert == 0, i1, jnp.where(expert == 1, i2, 0))
    gate = jnp.where(expert == 0, g1, jnp.where(expert == 1, g2, 0.0))
    return idx, gate


def _gather_rows(table, idx):
    n, d = idx.shape[0], table.shape[1]
    workers = SC_CORES * SC_SUBCORES
    per_worker = n // workers
    pairs = per_worker // (2 * SC_WINDOW)
    assert n == workers * pairs * 2 * SC_WINDOW, (n, workers, SC_WINDOW)
    mesh = plsc.VectorSubcoreMesh(core_axis_name="core", subcore_axis_name="subcore",
                                  num_cores=SC_CORES, num_subcores=SC_SUBCORES)

    def gather_kernel(table_hbm, idx_hbm, out_hbm, idx_a, idx_b, rows_a, rows_b, sem_a, sem_b):
        worker = lax.axis_index("subcore") * SC_CORES + lax.axis_index("core")
        base = worker * per_worker

        @pl.loop(0, pairs)
        def _(p):
            r0 = base + p * (2 * SC_WINDOW)
            r1 = r0 + SC_WINDOW
            pltpu.sync_copy(idx_hbm.at[pl.ds(r0, SC_WINDOW)], idx_a)
            gather_a = pltpu.async_copy(table_hbm.at[idx_a], rows_a, sem_a)
            pltpu.sync_copy(idx_hbm.at[pl.ds(r1, SC_WINDOW)], idx_b)
            gather_b = pltpu.async_copy(table_hbm.at[idx_b], rows_b, sem_b)
            gather_a.wait()
            pltpu.sync_copy(rows_a, out_hbm.at[pl.ds(r0, SC_WINDOW)])
            gather_b.wait()
            pltpu.sync_copy(rows_b, out_hbm.at[pl.ds(r1, SC_WINDOW)])

    idx_buf = pltpu.VMEM((SC_WINDOW,), jnp.int32)
    row_buf = pltpu.VMEM((SC_WINDOW, d), table.dtype)
    return pl.kernel(gather_kernel, out_type=jax.ShapeDtypeStruct((n, d), table.dtype), mesh=mesh,
                     scratch_types=[idx_buf, idx_buf, row_buf, row_buf,
                                    pltpu.SemaphoreType.DMA, pltpu.SemaphoreType.DMA],
                     name="moe_row_gather")(table, idx)


def _expert_kernel(meta_ref, xs_ref, wgu_ref, wd_ref, ys_ref, *, n_blocks):
    @pl.when(pl.program_id(0) < meta_ref[n_blocks])
    def _():
        h = _unpack_halves(xs_ref[...])
        acc = jnp.zeros((MOE_ROWS, D_MODEL), _F32)
        for c in range(D_FF_EXPERT // FF_CHUNK_EXPERT):
            lo = c * FF_CHUNK_EXPERT
            gate = _dot(h, wgu_ref[0, :, lo:lo + FF_CHUNK_EXPERT])
            up = _dot(h, wgu_ref[0, :, D_FF_EXPERT + lo:D_FF_EXPERT + lo + FF_CHUNK_EXPERT])
            a = (_silu(gate) * up).astype(_BF16)
            acc = acc + _dot(a, wd_ref[0, lo:lo + FF_CHUNK_EXPERT, :])
        ys_ref[...] = _pack_halves(acc)


def _experts(meta, xs, w_gu, w_down):
    n_rows, half = xs.shape
    n_blocks = n_rows // MOE_ROWS
    grid_spec = pltpu.PrefetchScalarGridSpec(
        num_scalar_prefetch=1,
        grid=(n_blocks,),
        in_specs=[pl.BlockSpec((MOE_ROWS, half), lambda i, m: (i, 0)),
                  pl.BlockSpec((1, D_MODEL, 2 * D_FF_EXPERT), lambda i, m: (m[i], 0, 0),
                               pipeline_mode=pl.Buffered(1)),
                  pl.BlockSpec((1, D_FF_EXPERT, D_MODEL), lambda i, m: (m[i], 0, 0),
                               pipeline_mode=pl.Buffered(1))],
        out_specs=pl.BlockSpec((MOE_ROWS, half), lambda i, m: (i, 0)),
    )
    return pl.pallas_call(
        functools.partial(_expert_kernel, n_blocks=n_blocks),
        grid_spec=grid_spec,
        out_shape=jax.ShapeDtypeStruct((n_rows, half), jnp.uint32),
        compiler_params=_params("arbitrary"),
        name="moe_experts",
    )(meta, xs, w_gu, w_down)


def _combine_kernel(x_ref, y0_ref, y1_ref, gate_ref, g_ref, o_ref, *, tc):
    gates = gate_ref[...]
    pad = jnp.zeros((LANES - SUBLANES, LANES), _F32)
    cols = [jnp.transpose(jnp.concatenate([gates[:, c * LANES:(c + 1) * LANES], pad], axis=0))
            for c in range(tc // LANES)]
    gcol = jnp.concatenate(cols, axis=0)
    y0 = _unpack_halves(y0_ref[...]).astype(_F32)
    y1 = _unpack_halves(y1_ref[...]).astype(_F32)
    f = y0 * gcol[:, 0:1] + y1 * gcol[:, 1:2]
    o_ref[...] = _rms(x_ref[...] + f, g_ref[...])


def _combine(x2, y0, y1, gates, g_final, *, tc):
    t = x2.shape[0]
    half = y0.shape[1]
    return pl.pallas_call(
        functools.partial(_combine_kernel, tc=tc),
        grid=(t // tc,),
        in_specs=[pl.BlockSpec((tc, D_MODEL), lambda i: (i, 0)),
                  pl.BlockSpec((tc, half), lambda i: (i, 0)),
                  pl.BlockSpec((tc, half), lambda i: (i, 0)),
                  pl.BlockSpec((SUBLANES, tc), lambda i: (0, i)),
                  _const_spec((1, D_MODEL))],
        out_specs=pl.BlockSpec((tc, D_MODEL), lambda i: (i, 0)),
        out_shape=jax.ShapeDtypeStruct(x2.shape, _F32),
        compiler_params=_params("parallel"),
        name="moe_combine",
    )(x2, y0, y1, gates, g_final)


def _routing_offsets(expert_idx, n_blocks):
    t = expert_idx.shape[1]
    n = 2 * t
    flat_e = expert_idx[:2].reshape(n)
    experts = jnp.arange(N_EXPERTS, dtype=jnp.int32)
    onehot = (flat_e[:, None] == experts[None, :]).astype(jnp.int32)
    running = jnp.cumsum(onehot, axis=0)
    rank = jnp.sum((running - onehot) * onehot, axis=1)
    counts = running[-1]
    padded = (counts + MOE_ROWS - 1) // MOE_ROWS * MOE_ROWS
    pad_end = jnp.cumsum(padded)
    pad_start = pad_end - padded
    start = jnp.cumsum(counts) - counts
    dest = (jnp.sum(onehot * pad_start[None, :], axis=1) + rank).astype(jnp.int32)

    block_row = jnp.arange(n_blocks, dtype=jnp.int32) * MOE_ROWS
    block_expert = jnp.minimum(jnp.sum((block_row[:, None] >= pad_end[None, :]).astype(jnp.int32), axis=1),
                               N_EXPERTS - 1)
    meta = jnp.concatenate([block_expert, pad_end[-1:] // MOE_ROWS]).astype(jnp.int32)

    order = jnp.argsort(flat_e, stable=True).astype(jnp.int32)
    row_onehot = jnp.repeat((block_expert[:, None] == experts[None, :]).astype(jnp.int32), MOE_ROWS, axis=0)
    off = jnp.arange(n_blocks * MOE_ROWS, dtype=jnp.int32) - jnp.sum(row_onehot * pad_start[None, :], axis=1)
    valid = off < jnp.sum(row_onehot * counts[None, :], axis=1)
    pos = jnp.clip(jnp.sum(row_onehot * start[None, :], axis=1) + off, 0, n - 1)
    src_tok = jnp.where(valid, order[pos] % t, 0).astype(jnp.int32)
    return dest, src_tok, meta


def _dense_layers(x, mem, w, w_experts=None, run_after=None):
    b, s, d = x.shape
    t = b * s

    k0, v0 = _mem_kv(mem, w['norm_mem'][0], w['w_mem_kv'][0])
    x = _mixer_a(x, k0, v0, w['norm_mix'][0], w['w_in_a'], w['g_v_a'], w['w_s_a'], w['b_s_a'],
                 w['w_out'][0], tm=ROW_TILE)
    side = () if w_experts is not None else (w['w_exp_gu'], w['w_exp_down'])
    x2, cast = _ffn(x.reshape(t, d), w['norm_ffn'][0], w['w_ffn_gu'], w['w_ffn_down'], side, tm=ROW_TILE)
    w_exp_gu, w_exp_down = w_experts if w_experts is not None else cast

    k1, v1 = _mem_kv(mem, w['norm_mem'][1], w['w_mem_kv'][1])
    proj = _proj_b(x2, w['norm_mix'][1], w['w_in_b'], tm=ROW_TILE).reshape(b, s, -1)
    tok = _natten(proj, _natten_bias_table(w['rpb_b'], s // GRID_W), run_after)
    x, h_packed, expert_idx, gates = _mixer_b_out(x2.reshape(b, s, d), tok, proj, k1, v1, w['w_out'][1],
                                                  w['norm_ffn'][1], w['w_router'], tm=ROW_TILE)
    return x.reshape(t, d), h_packed, expert_idx, gates, (w_exp_gu, w_exp_down)


def _dispatch(h_packed, expert_idx):
    t = h_packed.shape[0]
    n_blocks = 2 * t // MOE_ROWS + N_EXPERTS
    dest, src_tok, meta = _routing_offsets(expert_idx, n_blocks)
    return _gather_rows(h_packed, src_tok), dest, src_tok, meta


def _experts_combine(x2, gates, xs, dest, meta, w_experts, g_final):
    t = x2.shape[0]
    ys = _experts(meta, xs, *w_experts)
    y0 = _gather_rows(ys, dest[:t])
    y1 = _gather_rows(ys, dest[t:])
    return _combine(x2, y0, y1, gates, g_final, tc=ROW_TILE)


def _cast_kernel(w_ref, o_ref):
    o_ref[...] = w_ref[...].astype(o_ref.dtype)


def _to_bf16(w, rows_per_step):
    n, r, c = w.shape
    assert r % rows_per_step == 0
    block = (1, rows_per_step, c)
    return pl.pallas_call(
        _cast_kernel,
        grid=(n, r // rows_per_step),
        in_specs=[pl.BlockSpec(block, lambda i, j: (i, j, 0))],
        out_specs=pl.BlockSpec(block, lambda i, j: (i, j, 0)),
        out_shape=jax.ShapeDtypeStruct(w.shape, _BF16),
        compiler_params=_params("parallel", "parallel"),
        name="weight_cast",
    )(w)


def _split_router_weight(w):
    head = w.astype(_BF16)
    rest = (w - head.astype(_F32)).astype(_BF16)
    pad = ((0, 0), (0, LANES - N_EXPERTS))
    return jnp.concatenate([jnp.pad(head, pad), jnp.pad(rest, pad)], axis=1)


def _prepare_weights(norm_mix, norm_mem, norm_ffn, norm_final, w_in_a, g_v_a, w_s_a, b_s_a, w_in_b, rpb_b,
                     w_mem_kv, w_out, w_ffn_gu, w_ffn_down, w_router, w_exp_gu, w_exp_down):
    assert w_in_a.shape[0] == 1 and w_in_b.shape[0] == 1 and norm_mix.shape[0] == 2
    row = lambda a: a.astype(_F32)[:, None, :]
    return {
        'norm_mix': row(norm_mix), 'norm_mem': row(norm_mem), 'norm_ffn': row(norm_ffn),
        'norm_final': norm_final.astype(_F32)[None, :],
        'w_in_a': w_in_a[0].astype(_BF16),
        'g_v_a': g_v_a.astype(_F32),
        'w_s_a': jnp.concatenate([w_s_a[0, 0::2], w_s_a[0, 1::2]], axis=2).astype(_BF16),
        'b_s_a': jnp.repeat(b_s_a[0].astype(_F32).T, HEAD_DIM, axis=1),
        'w_in_b': w_in_b[0].astype(_BF16),
        'rpb_b': rpb_b[0],
        'w_mem_kv': w_mem_kv.astype(_BF16),
        'w_out': w_out.astype(_BF16),
        'w_ffn_gu': _to_bf16(w_ffn_gu, 256)[0],
        'w_ffn_down': _to_bf16(w_ffn_down, 704)[0],
        'w_router': _split_router_weight(w_router[0].astype(_F32)),
        'w_exp_gu': w_exp_gu[0].astype(_F32),
        'w_exp_down': w_exp_down[0].astype(_F32),
    }


def kernel(x_prompt, x_sample, mem_prompt, mem_sample, norm_mix, norm_mem, norm_ffn, norm_final, w_in_a, g_v_a, w_s_a, b_s_a, w_in_b, rpb_b, w_mem_kv, w_out, w_ffn_gu, w_ffn_down, w_router, w_exp_gu, w_exp_down):
    w = _prepare_weights(norm_mix, norm_mem, norm_ffn, norm_final, w_in_a, g_v_a, w_s_a, b_s_a, w_in_b, rpb_b,
                         w_mem_kv, w_out, w_ffn_gu, w_ffn_down, w_router, w_exp_gu, w_exp_down)
    x2_p, h_p, idx_p, gates_p, w_experts = _dense_layers(x_prompt, mem_prompt, w)
    xs_p, dest_p, _, meta_p = _dispatch(h_p, idx_p)
    x2_s, h_s, idx_s, gates_s, _ = _dense_layers(x_sample, mem_sample, w, w_experts, run_after=xs_p)
    xs_s, dest_s, _, meta_s = _dispatch(h_s, idx_s)
    y_prompt = _experts_combine(x2_p, gates_p, xs_p, dest_p, meta_p, w_experts, w['norm_final'])
    y_sample = _experts_combine(x2_s, gates_s, xs_s, dest_s, meta_s, w_experts, w['norm_final'])
    return (y_prompt.reshape(x_prompt.shape), y_sample.reshape(x_sample.shape))
```

```python
import functools

import numpy as np
import jax
import jax.numpy as jnp
from jax import lax
from jax.experimental import pallas as pl
from jax.experimental.pallas import tpu as pltpu
from jax.experimental.pallas import tpu_sc as plsc

D_MODEL = 1024
HEAD_DIM = 64
XA_HEADS = 4
XA_WIDTH = XA_HEADS * HEAD_DIM
TOK_WIDTH = D_MODEL - XA_WIDTH
CHUNK = 128
N_PAIR = TOK_WIDTH // (2 * HEAD_DIM)
GRID_W = 64
WIN_H = 8
WIN_W = 16
N_MEM = 256
D_FF = 2816
N_EXPERTS = 8
D_FF_EXPERT = 3584
RMS_EPS = 1e-6
NEG_INF = -1e30
ATTN_SCALE = HEAD_DIM ** -0.5

LANES = 128
SUBLANES = 8
VMEM_LIMIT = 56 * 1024 * 1024

NAT_ROWS_PER_STEP = 2
NAT_WIN_ROWS = WIN_H + NAT_ROWS_PER_STEP - 1
NAT_KEYS = NAT_WIN_ROWS * GRID_W
NAT_PATTERNS = 5
NAT_UNROLL = 4

MOE_ROWS = 512
SC_CORES = 2
SC_SUBCORES = 16
SC_WINDOW = 64
MXU_WIDTH = 256
FF_SPLIT_DENSE = 6 * MXU_WIDTH
ROW_TILE = 512
FF_CHUNK_EXPERT = 512

_BF16 = jnp.bfloat16
_F32 = jnp.float32
_NT = (((1,), (1,)), ((), ()))
_TN = (((0,), (0,)), ((), ()))


def _params(*semantics):
    return pltpu.CompilerParams(dimension_semantics=semantics, vmem_limit_bytes=VMEM_LIMIT)


def _const_spec(shape):
    zeros = (0,) * len(shape)
    return pl.BlockSpec(shape, lambda *_: zeros, pipeline_mode=pl.Buffered(1))


def _rms(x, g):
    ms = jnp.mean(x * x, axis=-1, keepdims=True)
    return x * lax.rsqrt(ms + RMS_EPS) * g


def _dot(a, b):
    return jnp.dot(a, b, preferred_element_type=_F32)


def _gelu(x):
    return 0.5 * x * (1.0 + lax.erf(x * np.float32(1.0 / np.sqrt(2.0))))


def _silu(x):
    return x / (1.0 + jnp.exp(-x))


def _softmax_rows(s):
    m = jnp.max(s, axis=-1, keepdims=True)
    e = jnp.exp(s - m)
    return e / jnp.sum(e, axis=-1, keepdims=True)


def _xattn_outproj(x, tok, qx, k, v, wout_ref):
    lane = lax.broadcasted_iota(jnp.int32, (1, XA_WIDTH), 1)
    qx = qx.astype(_F32)
    xa = jnp.zeros(qx.shape, _F32)
    for h in range(XA_HEADS):
        in_head = (lane >= h * HEAD_DIM) & (lane < (h + 1) * HEAD_DIM)
        qh = jnp.where(in_head, qx, 0.0).astype(_BF16)
        s = lax.dot_general(qh, k, _NT, preferred_element_type=_F32)
        p = _softmax_rows(s).astype(_BF16)
        xa = jnp.where(in_head, _dot(p, v), xa)
    y = _dot(tok, wout_ref[:TOK_WIDTH, :]) + _dot(xa.astype(_BF16), wout_ref[TOK_WIDTH:, :])
    return x + y


def _mem_kv_kernel(mem_ref, g_ref, w_ref, k_ref, v_ref):
    m = _rms(mem_ref[0], g_ref[...]).astype(_BF16)
    kv = _dot(m, w_ref[...])
    k_ref[0] = kv[:, :XA_WIDTH].astype(_BF16)
    v_ref[0] = kv[:, XA_WIDTH:].astype(_BF16)


def _mem_kv(mem, g, w_kv):
    b = mem.shape[0]
    out = jax.ShapeDtypeStruct((b, N_MEM, XA_WIDTH), _BF16)
    return pl.pallas_call(
        _mem_kv_kernel,
        grid=(b,),
        in_specs=[pl.BlockSpec((1, N_MEM, D_MODEL), lambda i: (i, 0, 0)),
                  _const_spec((1, D_MODEL)),
                  _const_spec((D_MODEL, 2 * XA_WIDTH))],
        out_specs=[pl.BlockSpec((1, N_MEM, XA_WIDTH), lambda i: (i, 0, 0))] * 2,
        out_shape=[out, out],
        compiler_params=_params("parallel"),
        name="mem_kv",
    )(mem, g, w_kv)


def _mixer_a_kernel(x_ref, k_ref, v_ref, g_ref, win_ref, gv_ref, ws_ref, bs_ref, wout_ref, o_ref, *, tm):
    assert tm % (2 * CHUNK) == 0
    x = x_ref[0]
    h = _rms(x, g_ref[...]).astype(_BF16)
    proj = _dot(h, win_ref[...])
    z = _gelu(proj[:, :2 * TOK_WIDTH])
    u = z[:, :TOK_WIDTH]
    vn = _rms(z[:, TOK_WIDTH:], gv_ref[...]).astype(_BF16)
    low_half = lax.broadcasted_iota(jnp.int32, (1, LANES), 1) < HEAD_DIM
    bias = bs_ref[...]
    zero = jnp.zeros((CHUNK, LANES), _BF16)
    toks = []
    for c in range(0, tm // CHUNK, 2):
        va, vb = vn[c * CHUNK:(c + 1) * CHUNK], vn[(c + 1) * CHUNK:(c + 2) * CHUNK]
        cols_a, cols_b = [], []
        for j in range(N_PAIR):
            pa, pb = va[:, j * LANES:(j + 1) * LANES], vb[:, j * LANES:(j + 1) * LANES]
            even = jnp.concatenate([jnp.where(low_half, pa, zero), jnp.where(low_half, pb, zero)], axis=1)
            odd = jnp.concatenate([jnp.where(low_half, zero, pa), jnp.where(low_half, zero, pb)], axis=1)
            r = _dot(ws_ref[j], jnp.concatenate([even, odd], axis=0))
            cols_a.append(r[:, :LANES])
            cols_b.append(r[:, LANES:])
        for side, cols in enumerate((cols_a, cols_b)):
            sv = jnp.concatenate(cols, axis=1) + bias
            toks.append(u[(c + side) * CHUNK:(c + side + 1) * CHUNK] * sv)
    tok = jnp.concatenate(toks, axis=0).astype(_BF16)
    qx = proj[:, 2 * TOK_WIDTH:] * ATTN_SCALE
    o_ref[0] = _xattn_outproj(x, tok, qx, k_ref[0], v_ref[0], wout_ref)


def _mixer_a(x, k, v, g, w_in, g_v, w_s2, b_s2, w_out, *, tm):
    b, s, _ = x.shape
    return pl.pallas_call(
        functools.partial(_mixer_a_kernel, tm=tm),
        grid=(b, s // tm),
        in_specs=[pl.BlockSpec((1, tm, D_MODEL), lambda i, j: (i, j, 0)),
                  pl.BlockSpec((1, N_MEM, XA_WIDTH), lambda i, j: (i, 0, 0)),
                  pl.BlockSpec((1, N_MEM, XA_WIDTH), lambda i, j: (i, 0, 0)),
                  _const_spec((1, D_MODEL)),
                  _const_spec((D_MODEL, 2 * TOK_WIDTH + XA_WIDTH)),
                  _const_spec((1, TOK_WIDTH)),
                  _const_spec((N_PAIR, CHUNK, 2 * CHUNK)),
                  _const_spec((CHUNK, TOK_WIDTH)),
                  _const_spec((D_MODEL, D_MODEL))],
        out_specs=pl.BlockSpec((1, tm, D_MODEL), lambda i, j: (i, j, 0)),
        out_shape=jax.ShapeDtypeStruct(x.shape, _F32),
        compiler_params=_params("parallel", "parallel"),
        name="mixer_gmlp",
    )(x, k, v, g, w_in, g_v, w_s2, b_s2, w_out)


def _ffn_kernel(x_ref, g_ref, wgu_ref, wd_ref, *refs):
    n_side = (len(refs) - 1) // 2
    o_ref = refs[n_side]
    x = x_ref[...]
    h = _rms(x, g_ref[...]).astype(_BF16)
    acc = x
    for lo, hi in ((0, FF_SPLIT_DENSE), (FF_SPLIT_DENSE, D_FF)):
        gate = _dot(h, wgu_ref[:, lo:hi])
        up = _dot(h, wgu_ref[:, D_FF + lo:D_FF + hi])
        a = (_silu(gate) * up).astype(_BF16)
        acc = acc + _dot(a, wd_ref[lo:hi, :])
    o_ref[...] = acc
    for src, dst in zip(refs[:n_side], refs[n_side + 1:]):
        dst[...] = src[...].astype(dst.dtype)


def _ffn(x2, g, w_gu, w_down, side_casts=(), *, tm):
    t = x2.shape[0]
    steps = t // tm
    flat = [w.reshape(-1, w.shape[-1]) for w in side_casts]
    slabs = [(w.shape[0] // steps, w.shape[1]) for w in flat]
    assert all(w.shape[0] == s[0] * steps and s[0] % (2 * SUBLANES) == 0 for w, s in zip(flat, slabs))
    side_specs = [pl.BlockSpec(s, lambda i: (i, 0)) for s in slabs]
    outs = pl.pallas_call(
        _ffn_kernel,
        grid=(steps,),
        in_specs=[pl.BlockSpec((tm, D_MODEL), lambda i: (i, 0)),
                  _const_spec((1, D_MODEL)),
                  _const_spec((D_MODEL, 2 * D_FF)),
                  _const_spec((D_FF, D_MODEL))] + side_specs,
        out_specs=[pl.BlockSpec((tm, D_MODEL), lambda i: (i, 0))] + side_specs,
        out_shape=[jax.ShapeDtypeStruct(x2.shape, _F32)] + [jax.ShapeDtypeStruct(w.shape, _BF16) for w in flat],
        compiler_params=_params("parallel"),
        name="ffn_dense",
    )(x2, g, w_gu, w_down, *flat)
    return outs[0], [o.reshape(w.shape) for o, w in zip(outs[1:], side_casts)]


def _proj_b_kernel(x_ref, g_ref, w_ref, o_ref):
    h = _rms(x_ref[...], g_ref[...]).astype(_BF16)
    proj = _dot(h, w_ref[...])
    col = lax.broadcasted_iota(jnp.int32, (1, proj.shape[1]), 1)
    is_query = (col < TOK_WIDTH) | (col >= 3 * TOK_WIDTH)
    o_ref[...] = (proj * jnp.where(is_query, ATTN_SCALE, 1.0)).astype(_BF16)


def _proj_b(x2, g, w_in, *, tm):
    t = x2.shape[0]
    n = w_in.shape[1]
    return pl.pallas_call(
        _proj_b_kernel,
        grid=(t // tm,),
        in_specs=[pl.BlockSpec((tm, D_MODEL), lambda i: (i, 0)),
                  _const_spec((1, D_MODEL)),
                  _const_spec((D_MODEL, n))],
        out_specs=pl.BlockSpec((tm, n), lambda i: (i, 0)),
        out_shape=jax.ShapeDtypeStruct((t, n), _BF16),
        compiler_params=_params("parallel"),
        name="proj_natten",
    )(x2, g, w_in)


def _natten_pattern(step, n_steps):
    return jnp.where(step < 2, step, jnp.where(step >= n_steps - 2, step - n_steps + NAT_PATTERNS, 2))


def _natten_window_start(step, rows):
    first = jnp.clip(NAT_ROWS_PER_STEP * step - WIN_H // 2, 0, rows - WIN_H)
    return jnp.minimum(first, rows - NAT_WIN_ROWS)


def _natten_kernel(q_ref, k_ref, v_ref, tab_ref, *rest, rows):
    o_ref = rest[-1]
    n_steps = rows // NAT_ROWS_PER_STEP
    nq = NAT_ROWS_PER_STEP * GRID_W
    low_half = lax.broadcasted_iota(jnp.int32, (1, LANES), 1) < HEAD_DIM

    ones = jnp.ones((NAT_KEYS, LANES), _BF16)

    def one_step(step):
        key0 = pl.multiple_of(_natten_window_start(step, rows) * GRID_W, GRID_W)
        q0 = pl.multiple_of(step * nq, nq)
        kw = k_ref[0, pl.ds(key0, NAT_KEYS), :]
        vw = v_ref[0, pl.ds(key0, NAT_KEYS), :]
        q2 = q_ref[0, pl.ds(q0, nq), :]
        zero = jnp.zeros_like(q2)
        qm = jnp.concatenate([jnp.where(low_half, q2, zero), jnp.where(low_half, zero, q2)], axis=0)
        st = lax.dot_general(kw, qm, _NT, preferred_element_type=_F32)
        st = st + tab_ref[0, _natten_pattern(step, n_steps)]
        e = jnp.exp(st - jnp.max(st, axis=0, keepdims=True)).astype(_BF16)
        o2 = lax.dot_general(e, jnp.concatenate([vw, ones], axis=1), _TN,
                             preferred_element_type=_F32)
        o2 = o2[:, :LANES] / o2[:, LANES:]
        o_ref[0, pl.ds(q0, nq), :] = jnp.where(low_half, o2[:nq], o2[nq:]).astype(_BF16)

    def trip(i, carry):
        for u in range(NAT_UNROLL):
            one_step(i * NAT_UNROLL + u)
        return carry

    assert n_steps % NAT_UNROLL == 0
    lax.fori_loop(0, n_steps // NAT_UNROLL, trip, 0)


def _natten_bias_kernel(picked_ref, expand_ref, o_ref):
    o_ref[0] = jnp.dot(picked_ref[0], expand_ref[...], preferred_element_type=_F32,
                       precision=lax.Precision.HIGHEST)


def _natten_bias_table(rpb, rows):
    n_steps = rows // NAT_ROWS_PER_STEP
    assert rows % NAT_ROWS_PER_STEP == 0 and rows >= NAT_WIN_ROWS and n_steps >= NAT_PATTERNS
    steps = np.array([0, 1, 2, n_steps - 2, n_steps - 1])
    first = np.clip(NAT_ROWS_PER_STEP * steps - WIN_H // 2, 0, rows - WIN_H)
    win0 = np.minimum(first, rows - NAT_WIN_ROWS)
    i = np.arange(NAT_WIN_ROWS)[None, :, None, None, None]
    kc = np.arange(GRID_W)[None, None, :, None, None]
    j = np.arange(NAT_ROWS_PER_STEP)[None, None, None, :, None]
    qc = np.arange(GRID_W)[None, None, None, None, :]
    r = NAT_ROWS_PER_STEP * steps[:, None, None, None, None] + j
    key_row = win0[:, None, None, None, None] + i
    start = np.clip(r - WIN_H // 2, 0, rows - WIN_H)
    row_ok = (key_row >= start) & (key_row < start + WIN_H)
    col_start = np.clip(qc - WIN_W // 2, 0, GRID_W - WIN_W)
    col_ok = (kc >= col_start) & (kc < col_start + WIN_W)
    rel_row = np.clip(key_row - r + (WIN_H - 1), 0, 2 * WIN_H - 2)
    rel_col = np.clip(kc - qc + (WIN_W - 1), 0, 2 * WIN_W - 2)
    shape = (NAT_PATTERNS, NAT_WIN_ROWS, GRID_W, NAT_ROWS_PER_STEP, GRID_W)
    ok = np.broadcast_to(row_ok & col_ok, shape)
    n_heads, n_rel_col = rpb.shape[0], rpb.shape[2]
    row_sel = rel_row.reshape(-1)
    n_sel = -(-row_sel.size // SUBLANES) * SUBLANES
    picked = jnp.zeros((n_heads, n_sel, LANES), _F32)
    picked = picked.at[:, :row_sel.size, :n_rel_col].set(rpb.astype(_F32)[:, row_sel, :])
    expand = np.zeros((LANES, GRID_W * GRID_W), np.float32)
    expand[rel_col.reshape(-1), np.arange(GRID_W * GRID_W)] = 1.0
    tab = pl.pallas_call(
        _natten_bias_kernel,
        grid=(n_heads,),
        in_specs=[pl.BlockSpec((1, n_sel, LANES), lambda h: (h, 0, 0)),
                  _const_spec(expand.shape)],
        out_specs=pl.BlockSpec((1, n_sel, GRID_W * GRID_W), lambda h: (h, 0, 0)),
        out_shape=jax.ShapeDtypeStruct((n_heads, n_sel, GRID_W * GRID_W), _F32),
        compiler_params=_params("parallel"),
        name="natten_bias",
    )(picked, jnp.asarray(expand))
    nq = NAT_ROWS_PER_STEP * GRID_W
    tab = tab[:, :row_sel.size].reshape(N_PAIR, 2, NAT_PATTERNS, NAT_WIN_ROWS, NAT_ROWS_PER_STEP, GRID_W, GRID_W)
    tab = tab.transpose(0, 2, 3, 5, 1, 4, 6)
    tab = jnp.where(ok[None, :, :, :, None], tab, NEG_INF)
    return tab.reshape(N_PAIR, NAT_PATTERNS, NAT_KEYS, 2 * nq)


def _natten(proj3, table, run_after=None):
    b, s, _ = proj3.shape
    rows = s // GRID_W
    blk = (1, s, LANES)
    order_specs = [] if run_after is None else [pl.BlockSpec((SUBLANES, LANES), lambda p, i: (0, 0))]
    order_args = [] if run_after is None else [run_after]
    return pl.pallas_call(
        functools.partial(_natten_kernel, rows=rows),
        grid=(N_PAIR, b),
        in_specs=[pl.BlockSpec(blk, lambda p, i: (i, 0, p)),
                  pl.BlockSpec(blk, lambda p, i: (i, 0, N_PAIR + p)),
                  pl.BlockSpec(blk, lambda p, i: (i, 0, 2 * N_PAIR + p)),
                  pl.BlockSpec((1,) + table.shape[1:], lambda p, i: (p, 0, 0, 0))] + order_specs,
        out_specs=pl.BlockSpec(blk, lambda p, i: (i, 0, p)),
        out_shape=jax.ShapeDtypeStruct((b, s, TOK_WIDTH), _BF16),
        compiler_params=_params("parallel", "parallel"),
        name="natten",
    )(proj3, proj3, proj3, table, *order_args)


def _mixer_b_out_kernel(x_ref, tok_ref, qx_ref, k_ref, v_ref, wout_ref, g_ref, wr_ref,
                        o_ref, h_ref, idx_ref, gate_ref):
    x = _xattn_outproj(x_ref[0], tok_ref[0], qx_ref[0], k_ref[0], v_ref[0], wout_ref)
    o_ref[0] = x
    h = _rms(x, g_ref[...])
    h_ref[...] = _pack_halves(h)
    idx_ref[...], gate_ref[...] = _route(h, wr_ref)


def _mixer_b_out(x, tok, proj3, k, v, w_out, g_ffn, w_router2, *, tm):
    b, s, _ = x.shape
    t = b * s
    nj = s // tm
    qx_block = 3 * TOK_WIDTH // XA_WIDTH
    return pl.pallas_call(
        _mixer_b_out_kernel,
        grid=(b, nj),
        in_specs=[pl.BlockSpec((1, tm, D_MODEL), lambda i, j: (i, j, 0)),
                  pl.BlockSpec((1, tm, TOK_WIDTH), lambda i, j: (i, j, 0)),
                  pl.BlockSpec((1, tm, XA_WIDTH), lambda i, j: (i, j, qx_block)),
                  pl.BlockSpec((1, N_MEM, XA_WIDTH), lambda i, j: (i, 0, 0)),
                  pl.BlockSpec((1, N_MEM, XA_WIDTH), lambda i, j: (i, 0, 0)),
                  _const_spec((D_MODEL, D_MODEL)),
                  _const_spec((1, D_MODEL)),
                  _const_spec((D_MODEL, 2 * LANES))],
        out_specs=[pl.BlockSpec((1, tm, D_MODEL), lambda i, j: (i, j, 0)),
                   pl.BlockSpec((tm, D_MODEL // 2), lambda i, j: (i * nj + j, 0)),
                   pl.BlockSpec((SUBLANES, tm), lambda i, j: (0, i * nj + j)),
                   pl.BlockSpec((SUBLANES, tm), lambda i, j: (0, i * nj + j))],
        out_shape=[jax.ShapeDtypeStruct(x.shape, _F32),
                   jax.ShapeDtypeStruct((t, D_MODEL // 2), jnp.uint32),
                   jax.ShapeDtypeStruct((SUBLANES, t), jnp.int32),
                   jax.ShapeDtypeStruct((SUBLANES, t), _F32)],
        compiler_params=_params("parallel", "parallel"),
        name="mixer_natten_out",
    )(x, tok, proj3, k, v, w_out, g_ffn, w_router2)


def _pack_halves(h):
    half = D_MODEL // 2
    hi = pltpu.bitcast(h[:, :half].astype(_BF16).astype(_F32), jnp.uint32)
    lo = pltpu.bitcast(h[:, half:].astype(_BF16).astype(_F32), jnp.uint32)
    return hi | (lo >> 16)


def _unpack_halves(w):
    hi = pltpu.bitcast(w & jnp.uint32(0xFFFF0000), _F32).astype(_BF16)
    lo = pltpu.bitcast(w << 16, _F32).astype(_BF16)
    return jnp.concatenate([hi, lo], axis=1)


def _route(h, wr_ref):
    h_head = h.astype(_BF16)
    h_rest = (h - h_head.astype(_F32)).astype(_BF16)
    both = _dot(h_head, wr_ref[...])
    logits = both[:, :LANES] + both[:, LANES:] + _dot(h_rest, wr_ref[:, :LANES])
    assert N_EXPERTS == SUBLANES
    lt = jnp.concatenate([jnp.transpose(logits[c * LANES:(c + 1) * LANES])[:SUBLANES]
                          for c in range(h.shape[0] // LANES)], axis=1)
    expert = lax.broadcasted_iota(jnp.int32, lt.shape, 0)
    v1 = jnp.max(lt, axis=0, keepdims=True)
    i1 = jnp.min(jnp.where(lt == v1, expert, N_EXPERTS), axis=0, keepdims=True)
    lt2 = jnp.where(expert == i1, -jnp.inf, lt)
    v2 = jnp.max(lt2, axis=0, keepdims=True)
    i2 = jnp.min(jnp.where(lt2 == v2, expert, N_EXPERTS), axis=0, keepdims=True)
    e2 = jnp.exp(v2 - v1)
    g1 = 1.0 / (1.0 + e2)
    g2 = e2 / (1.0 + e2)
    idx = jnp.where(expert == 0, i1, jnp.where(expert == 1, i2, 0))
    gate = jnp.where(expert == 0, g1, jnp.where(expert == 1, g2, 0.0))
    return idx, gate


def _gather_rows(table, idx):
    n, d = idx.shape[0], table.shape[1]
    workers = SC_CORES * SC_SUBCORES
    per_worker = n // workers
    pairs = per_worker // (2 * SC_WINDOW)
    assert n == workers * pairs * 2 * SC_WINDOW, (n, workers, SC_WINDOW)
    mesh = plsc.VectorSubcoreMesh(core_axis_name="core", subcore_axis_name="subcore",
                                  num_cores=SC_CORES, num_subcores=SC_SUBCORES)

    def gather_kernel(table_hbm, idx_hbm, out_hbm, idx_a, idx_b, rows_a, rows_b, sem_a, sem_b):
        worker = lax.axis_index("subcore") * SC_CORES + lax.axis_index("core")
        base = worker * per_worker

        @pl.loop(0, pairs)
        def _(p):
            r0 = base + p * (2 * SC_WINDOW)
            r1 = r0 + SC_WINDOW
            pltpu.sync_copy(idx_hbm.at[pl.ds(r0, SC_WINDOW)], idx_a)
            gather_a = pltpu.async_copy(table_hbm.at[idx_a], rows_a, sem_a)
            pltpu.sync_copy(idx_hbm.at[pl.ds(r1, SC_WINDOW)], idx_b)
            gather_b = pltpu.async_copy(table_hbm.at[idx_b], rows_b, sem_b)
            gather_a.wait()
            pltpu.sync_copy(rows_a, out_hbm.at[pl.ds(r0, SC_WINDOW)])
            gather_b.wait()
            pltpu.sync_copy(rows_b, out_hbm.at[pl.ds(r1, SC_WINDOW)])

    idx_buf = pltpu.VMEM((SC_WINDOW,), jnp.int32)
    row_buf = pltpu.VMEM((SC_WINDOW, d), table.dtype)
    return pl.kernel(gather_kernel, out_type=jax.ShapeDtypeStruct((n, d), table.dtype), mesh=mesh,
                     scratch_types=[idx_buf, idx_buf, row_buf, row_buf,
                                    pltpu.SemaphoreType.DMA, pltpu.SemaphoreType.DMA],
                     name="moe_row_gather")(table, idx)


def _expert_kernel(meta_ref, xs_ref, wgu_ref, wd_ref, ys_ref, *, n_blocks):
    @pl.when(pl.program_id(0) < meta_ref[n_blocks])
    def _():
        h = _unpack_halves(xs_ref[...])
        acc = jnp.zeros((MOE_ROWS, D_MODEL), _F32)
        for c in range(D_FF_EXPERT // FF_CHUNK_EXPERT):
            lo = c * FF_CHUNK_EXPERT
            gate = _dot(h, wgu_ref[0, :, lo:lo + FF_CHUNK_EXPERT])
            up = _dot(h, wgu_ref[0, :, D_FF_EXPERT + lo:D_FF_EXPERT + lo + FF_CHUNK_EXPERT])
            a = (_silu(gate) * up).astype(_BF16)
            acc = acc + _dot(a, wd_ref[0, lo:lo + FF_CHUNK_EXPERT, :])
        ys_ref[...] = _pack_halves(acc)


def _experts(meta, xs, w_gu, w_down):
    n_rows, half = xs.shape
    n_blocks = n_rows // MOE_ROWS
    grid_spec = pltpu.PrefetchScalarGridSpec(
        num_scalar_prefetch=1,
        grid=(n_blocks,),
        in_specs=[pl.BlockSpec((MOE_ROWS, half), lambda i, m: (i, 0)),
                  pl.BlockSpec((1, D_MODEL, 2 * D_FF_EXPERT), lambda i, m: (m[i], 0, 0),
                               pipeline_mode=pl.Buffered(1)),
                  pl.BlockSpec((1, D_FF_EXPERT, D_MODEL), lambda i, m: (m[i], 0, 0),
                               pipeline_mode=pl.Buffered(1))],
        out_specs=pl.BlockSpec((MOE_ROWS, half), lambda i, m: (i, 0)),
    )
    return pl.pallas_call(
        functools.partial(_expert_kernel, n_blocks=n_blocks),
        grid_spec=grid_spec,
        out_shape=jax.ShapeDtypeStruct((n_rows, half), jnp.uint32),
        compiler_params=_params("arbitrary"),
        name="moe_experts",
    )(meta, xs, w_gu, w_down)


def _combine_kernel(x_ref, y0_ref, y1_ref, gate_ref, g_ref, o_ref, *, tc):
    gates = gate_ref[...]
    pad = jnp.zeros((LANES - SUBLANES, LANES), _F32)
    cols = [jnp.transpose(jnp.concatenate([gates[:, c * LANES:(c + 1) * LANES], pad], axis=0))
            for c in range(tc // LANES)]
    gcol = jnp.concatenate(cols, axis=0)
    y0 = _unpack_halves(y0_ref[...]).astype(_F32)
    y1 = _unpack_halves(y1_ref[...]).astype(_F32)
    f = y0 * gcol[:, 0:1] + y1 * gcol[:, 1:2]
    o_ref[...] = _rms(x_ref[...] + f, g_ref[...])


def _combine(x2, y0, y1, gates, g_final, *, tc):
    t = x2.shape[0]
    half = y0.shape[1]
    return pl.pallas_call(
        functools.partial(_combine_kernel, tc=tc),
        grid=(t // tc,),
        in_specs=[pl.BlockSpec((tc, D_MODEL), lambda i: (i, 0)),
                  pl.BlockSpec((tc, half), lambda i: (i, 0)),
                  pl.BlockSpec((tc, half), lambda i: (i, 0)),
                  pl.BlockSpec((SUBLANES, tc), lambda i: (0, i)),
                  _const_spec((1, D_MODEL))],
        out_specs=pl.BlockSpec((tc, D_MODEL), lambda i: (i, 0)),
        out_shape=jax.ShapeDtypeStruct(x2.shape, _F32),
        compiler_params=_params("parallel"),
        name="moe_combine",
    )(x2, y0, y1, gates, g_final)


def _routing_offsets(expert_idx, n_blocks):
    t = expert_idx.shape[1]
    n = 2 * t
    flat_e = expert_idx[:2].reshape(n)
    experts = jnp.arange(N_EXPERTS, dtype=jnp.int32)
    onehot = (flat_e[:, None] == experts[None, :]).astype(jnp.int32)
    running = jnp.cumsum(onehot, axis=0)
    rank = jnp.sum((running - onehot) * onehot, axis=1)
    counts = running[-1]
    padded = (counts + MOE_ROWS - 1) // MOE_ROWS * MOE_ROWS
    pad_end = jnp.cumsum(padded)
    pad_start = pad_end - padded
    start = jnp.cumsum(counts) - counts
    dest = (jnp.sum(onehot * pad_start[None, :], axis=1) + rank).astype(jnp.int32)

    block_row = jnp.arange(n_blocks, dtype=jnp.int32) * MOE_ROWS
    block_expert = jnp.minimum(jnp.sum((block_row[:, None] >= pad_end[None, :]).astype(jnp.int32), axis=1),
                               N_EXPERTS - 1)
    meta = jnp.concatenate([block_expert, pad_end[-1:] // MOE_ROWS]).astype(jnp.int32)

    order = jnp.argsort(flat_e, stable=True).astype(jnp.int32)
    row_onehot = jnp.repeat((block_expert[:, None] == experts[None, :]).astype(jnp.int32), MOE_ROWS, axis=0)
    off = jnp.arange(n_blocks * MOE_ROWS, dtype=jnp.int32) - jnp.sum(row_onehot * pad_start[None, :], axis=1)
    valid = off < jnp.sum(row_onehot * counts[None, :], axis=1)
    pos = jnp.clip(jnp.sum(row_onehot * start[None, :], axis=1) + off, 0, n - 1)
    src_tok = jnp.where(valid, order[pos] % t, 0).astype(jnp.int32)
    return dest, src_tok, meta


def _dense_layers(x, mem, w, w_experts=None, run_after=None):
    b, s, d = x.shape
    t = b * s

    k0, v0 = _mem_kv(mem, w['norm_mem'][0], w['w_mem_kv'][0])
    x = _mixer_a(x, k0, v0, w['norm_mix'][0], w['w_in_a'], w['g_v_a'], w['w_s_a'], w['b_s_a'],
                 w['w_out'][0], tm=ROW_TILE)
    side = () if w_experts is not None else (w['w_exp_gu'], w['w_exp_down'])
    x2, cast = _ffn(x.reshape(t, d), w['norm_ffn'][0], w['w_ffn_gu'], w['w_ffn_down'], side, tm=ROW_TILE)
    w_exp_gu, w_exp_down = w_experts if w_experts is not None else cast

    k1, v1 = _mem_kv(mem, w['norm_mem'][1], w['w_mem_kv'][1])
    proj = _proj_b(x2, w['norm_mix'][1], w['w_in_b'], tm=ROW_TILE).reshape(b, s, -1)
    tok = _natten(proj, _natten_bias_table(w['rpb_b'], s // GRID_W), run_after)
    x, h_packed, expert_idx, gates = _mixer_b_out(x2.reshape(b, s, d), tok, proj, k1, v1, w['w_out'][1],
                                                  w['norm_ffn'][1], w['w_router'], tm=ROW_TILE)
    return x.reshape(t, d), h_packed, expert_idx, gates, (w_exp_gu, w_exp_down)


def _dispatch(h_packed, expert_idx):
    t = h_packed.shape[0]
    n_blocks = 2 * t // MOE_ROWS + N_EXPERTS
    dest, src_tok, meta = _routing_offsets(expert_idx, n_blocks)
    return _gather_rows(h_packed, src_tok), dest, src_tok, meta


def _experts_combine(x2, gates, xs, dest, meta, w_experts, g_final):
    t = x2.shape[0]
    ys = _experts(meta, xs, *w_experts)
    y0 = _gather_rows(ys, dest[:t])
    y1 = _gather_rows(ys, dest[t:])
    return _combine(x2, y0, y1, gates, g_final, tc=ROW_TILE)


def _cast_kernel(w_ref, o_ref):
    o_ref[...] = w_ref[...].astype(o_ref.dtype)


def _to_bf16(w, rows_per_step):
    n, r, c = w.shape
    assert r % rows_per_step == 0
    block = (1, rows_per_step, c)
    return pl.pallas_call(
        _cast_kernel,
        grid=(n, r // rows_per_step),
        in_specs=[pl.BlockSpec(block, lambda i, j: (i, j, 0))],
        out_specs=pl.BlockSpec(block, lambda i, j: (i, j, 0)),
        out_shape=jax.ShapeDtypeStruct(w.shape, _BF16),
        compiler_params=_params("parallel", "parallel"),
        name="weight_cast",
    )(w)


def _split_router_weight(w):
    head = w.astype(_BF16)
    rest = (w - head.astype(_F32)).astype(_BF16)
    pad = ((0, 0), (0, LANES - N_EXPERTS))
    return jnp.concatenate([jnp.pad(head, pad), jnp.pad(rest, pad)], axis=1)


def _prepare_weights(norm_mix, norm_mem, norm_ffn, norm_final, w_in_a, g_v_a, w_s_a, b_s_a, w_in_b, rpb_b,
                     w_mem_kv, w_out, w_ffn_gu, w_ffn_down, w_router, w_exp_gu, w_exp_down):
    assert w_in_a.shape[0] == 1 and w_in_b.shape[0] == 1 and norm_mix.shape[0] == 2
    row = lambda a: a.astype(_F32)[:, None, :]
    return {
        'norm_mix': row(norm_mix), 'norm_mem': row(norm_mem), 'norm_ffn': row(norm_ffn),
        'norm_final': norm_final.astype(_F32)[None, :],
        'w_in_a': w_in_a[0].astype(_BF16),
        'g_v_a': g_v_a.astype(_F32),
        'w_s_a': jnp.concatenate([w_s_a[0, 0::2], w_s_a[0, 1::2]], axis=2).astype(_BF16),
        'b_s_a': jnp.repeat(b_s_a[0].astype(_F32).T, HEAD_DIM, axis=1),
        'w_in_b': w_in_b[0].astype(_BF16),
        'rpb_b': rpb_b[0],
        'w_mem_kv': w_mem_kv.astype(_BF16),
        'w_out': w_out.astype(_BF16),
        'w_ffn_gu': _to_bf16(w_ffn_gu, 256)[0],
        'w_ffn_down': _to_bf16(w_ffn_down, 704)[0],
        'w_router': _split_router_weight(w_router[0].astype(_F32)),
        'w_exp_gu': w_exp_gu[0].astype(_F32),
        'w_exp_down': w_exp_down[0].astype(_F32),
    }


def kernel(x_prompt, x_sample, mem_prompt, mem_sample, norm_mix, norm_mem, norm_ffn, norm_final, w_in_a, g_v_a, w_s_a, b_s_a, w_in_b, rpb_b, w_mem_kv, w_out, w_ffn_gu, w_ffn_down, w_router, w_exp_gu, w_exp_down):
    w = _prepare_weights(norm_mix, norm_mem, norm_ffn, norm_final, w_in_a, g_v_a, w_s_a, b_s_a, w_in_b, rpb_b,
                         w_mem_kv, w_out, w_ffn_gu, w_ffn_down, w_router, w_exp_gu, w_exp_down)
    x2_p, h_p, idx_p, gates_p, w_experts = _dense_layers(x_prompt, mem_prompt, w)
    xs_p, dest_p, _, meta_p = _dispatch(h_p, idx_p)
    x2_s, h_s, idx_s, gates_s, _ = _dense_layers(x_sample, mem_sample, w, w_experts, run_after=xs_p)
    xs_s, dest_s, _, meta_s = _dispatch(h_s, idx_s)
    y_prompt = _experts_combine(x2_p, gates_p, xs_p, dest_p, meta_p, w_experts, w['norm_final'])
    y_sample = _experts_combine(x2_s, gates_s, xs_s, dest_s, meta_s, w_experts, w['norm_final'])
    return (y_prompt.reshape(x_prompt.shape), y_sample.reshape(x_sample.shape))
```

```python
import functools

import numpy as np
import jax
import jax.numpy as jnp
from jax import lax
from jax.experimental import pallas as pl
from jax.experimental.pallas import tpu as pltpu
from jax.experimental.pallas import tpu_sc as plsc

D_MODEL = 1024
HEAD_DIM = 64
XA_HEADS = 4
XA_WIDTH = XA_HEADS * HEAD_DIM
TOK_WIDTH = D_MODEL - XA_WIDTH
CHUNK = 128
N_PAIR = TOK_WIDTH // (2 * HEAD_DIM)
GRID_W = 64
WIN_H = 8
WIN_W = 16
N_MEM = 256
D_FF = 2816
N_EXPERTS = 8
D_FF_EXPERT = 3584
RMS_EPS = 1e-6
NEG_INF = -1e30
ATTN_SCALE = HEAD_DIM ** -0.5

LANES = 128
SUBLANES = 8
VMEM_LIMIT = 56 * 1024 * 1024

NAT_ROWS_PER_STEP = 2
NAT_WIN_ROWS = WIN_H + NAT_ROWS_PER_STEP - 1
NAT_KEYS = NAT_WIN_ROWS * GRID_W
NAT_PATTERNS = 5
NAT_UNROLL = 4

MOE_ROWS = 512
SC_CORES = 2
SC_SUBCORES = 16
SC_WINDOW = 64
MXU_WIDTH = 256
FF_SPLIT_DENSE = 6 * MXU_WIDTH
ROW_TILE = 512
FF_CHUNK_EXPERT = 512

_BF16 = jnp.bfloat16
_F32 = jnp.float32
_NT = (((1,), (1,)), ((), ()))
_TN = (((0,), (0,)), ((), ()))


def _params(*semantics):
    return pltpu.CompilerParams(dimension_semantics=semantics, vmem_limit_bytes=VMEM_LIMIT)


def _const_spec(shape):
    zeros = (0,) * len(shape)
    return pl.BlockSpec(shape, lambda *_: zeros, pipeline_mode=pl.Buffered(1))


def _rms(x, g):
    ms = jnp.mean(x * x, axis=-1, keepdims=True)
    return x * lax.rsqrt(ms + RMS_EPS) * g


def _dot(a, b):
    return jnp.dot(a, b, preferred_element_type=_F32)


def _gelu(x):
    return 0.5 * x * (1.0 + lax.erf(x * np.float32(1.0 / np.sqrt(2.0))))


def _silu(x):
    return x / (1.0 + jnp.exp(-x))


def _softmax_rows(s):
    m = jnp.max(s, axis=-1, keepdims=True)
    e = jnp.exp(s - m)
    return e / jnp.sum(e, axis=-1, keepdims=True)


def _xattn_outproj(x, tok, qx, k, v, wout_ref):
    lane = lax.broadcasted_iota(jnp.int32, (1, XA_WIDTH), 1)
    qx = qx.astype(_F32)
    xa = jnp.zeros(qx.shape, _F32)
    for h in range(XA_HEADS):
        in_head = (lane >= h * HEAD_DIM) & (lane < (h + 1) * HEAD_DIM)
        qh = jnp.where(in_head, qx, 0.0).astype(_BF16)
        s = lax.dot_general(qh, k, _NT, preferred_element_type=_F32)
        p = _softmax_rows(s).astype(_BF16)
        xa = jnp.where(in_head, _dot(p, v), xa)
    y = _dot(tok, wout_ref[:TOK_WIDTH, :]) + _dot(xa.astype(_BF16), wout_ref[TOK_WIDTH:, :])
    return x + y


def _mem_kv_kernel(mem_ref, g_ref, w_ref, *kv_refs):
    mem = mem_ref[0]
    for layer in range(w_ref.shape[0]):
        m = _rms(mem, g_ref[layer]).astype(_BF16)
        kv = _dot(m, w_ref[layer])
        kv_refs[2 * layer][0] = kv[:, :XA_WIDTH].astype(_BF16)
        kv_refs[2 * layer + 1][0] = kv[:, XA_WIDTH:].astype(_BF16)


def _mem_kv(mem, g, w_kv):
    b = mem.shape[0]
    layers = w_kv.shape[0]
    out = jax.ShapeDtypeStruct((b, N_MEM, XA_WIDTH), _BF16)
    return pl.pallas_call(
        _mem_kv_kernel,
        grid=(b,),
        in_specs=[pl.BlockSpec((1, N_MEM, D_MODEL), lambda i: (i, 0, 0)),
                  _const_spec((layers, 1, D_MODEL)),
                  _const_spec((layers, D_MODEL, 2 * XA_WIDTH))],
        out_specs=[pl.BlockSpec((1, N_MEM, XA_WIDTH), lambda i: (i, 0, 0))] * (2 * layers),
        out_shape=[out] * (2 * layers),
        compiler_params=_params("parallel"),
        name="mem_kv",
    )(mem, g, w_kv)


def _mixer_a_kernel(x_ref, k_ref, v_ref, g_ref, win_ref, gv_ref, ws_ref, bs_ref, wout_ref, o_ref, *, tm):
    assert tm % (2 * CHUNK) == 0
    x = x_ref[0]
    h = _rms(x, g_ref[...]).astype(_BF16)
    proj = _dot(h, win_ref[...])
    z = _gelu(proj[:, :2 * TOK_WIDTH])
    u = z[:, :TOK_WIDTH]
    vn = _rms(z[:, TOK_WIDTH:], gv_ref[...]).astype(_BF16)
    low_half = lax.broadcasted_iota(jnp.int32, (1, LANES), 1) < HEAD_DIM
    bias = bs_ref[...]
    zero = jnp.zeros((CHUNK, LANES), _BF16)
    toks = []
    for c in range(0, tm // CHUNK, 2):
        va, vb = vn[c * CHUNK:(c + 1) * CHUNK], vn[(c + 1) * CHUNK:(c + 2) * CHUNK]
        cols_a, cols_b = [], []
        for j in range(N_PAIR):
            pa, pb = va[:, j * LANES:(j + 1) * LANES], vb[:, j * LANES:(j + 1) * LANES]
            even = jnp.concatenate([jnp.where(low_half, pa, zero), jnp.where(low_half, pb, zero)], axis=1)
            odd = jnp.concatenate([jnp.where(low_half, zero, pa), jnp.where(low_half, zero, pb)], axis=1)
            r = _dot(ws_ref[j], jnp.concatenate([even, odd], axis=0))
            cols_a.append(r[:, :LANES])
            cols_b.append(r[:, LANES:])
        for side, cols in enumerate((cols_a, cols_b)):
            sv = jnp.concatenate(cols, axis=1) + bias
            toks.append(u[(c + side) * CHUNK:(c + side + 1) * CHUNK] * sv)
    tok = jnp.concatenate(toks, axis=0).astype(_BF16)
    qx = proj[:, 2 * TOK_WIDTH:] * ATTN_SCALE
    o_ref[0] = _xattn_outproj(x, tok, qx, k_ref[0], v_ref[0], wout_ref)


def _mixer_a(x, k, v, g, w_in, g_v, w_s2, b_s2, w_out, *, tm):
    b, s, _ = x.shape
    return pl.pallas_call(
        functools.partial(_mixer_a_kernel, tm=tm),
        grid=(b, s // tm),
        in_specs=[pl.BlockSpec((1, tm, D_MODEL), lambda i, j: (i, j, 0)),
                  pl.BlockSpec((1, N_MEM, XA_WIDTH), lambda i, j: (i, 0, 0)),
                  pl.BlockSpec((1, N_MEM, XA_WIDTH), lambda i, j: (i, 0, 0)),
                  _const_spec((1, D_MODEL)),
                  _const_spec((D_MODEL, 2 * TOK_WIDTH + XA_WIDTH)),
                  _const_spec((1, TOK_WIDTH)),
                  _const_spec((N_PAIR, CHUNK, 2 * CHUNK)),
                  _const_spec((CHUNK, TOK_WIDTH)),
                  _const_spec((D_MODEL, D_MODEL))],
        out_specs=pl.BlockSpec((1, tm, D_MODEL), lambda i, j: (i, j, 0)),
        out_shape=jax.ShapeDtypeStruct(x.shape, _F32),
        compiler_params=_params("parallel", "parallel"),
        name="mixer_gmlp",
    )(x, k, v, g, w_in, g_v, w_s2, b_s2, w_out)


def _ffn_kernel(x_ref, g_ref, wgu_ref, wd_ref, *refs):
    n_side = (len(refs) - 1) // 2
    o_ref = refs[n_side]
    x = x_ref[...]
    h = _rms(x, g_ref[...]).astype(_BF16)
    acc = x
    for lo, hi in ((0, FF_SPLIT_DENSE), (FF_SPLIT_DENSE, D_FF)):
        gate = _dot(h, wgu_ref[:, lo:hi])
        up = _dot(h, wgu_ref[:, D_FF + lo:D_FF + hi])
        a = (_silu(gate) * up).astype(_BF16)
        acc = acc + _dot(a, wd_ref[lo:hi, :])
    o_ref[...] = acc
    for src, dst in zip(refs[:n_side], refs[n_side + 1:]):
        dst[...] = src[...].astype(dst.dtype)


def _ffn(x2, g, w_gu, w_down, side_casts=(), *, tm):
    t = x2.shape[0]
    steps = t // tm
    flat = [w.reshape(-1, w.shape[-1]) for w in side_casts]
    slabs = [(w.shape[0] // steps, w.shape[1]) for w in flat]
    assert all(w.shape[0] == s[0] * steps and s[0] % (2 * SUBLANES) == 0 for w, s in zip(flat, slabs))
    side_specs = [pl.BlockSpec(s, lambda i: (i, 0)) for s in slabs]
    outs = pl.pallas_call(
        _ffn_kernel,
        grid=(steps,),
        in_specs=[pl.BlockSpec((tm, D_MODEL), lambda i: (i, 0)),
                  _const_spec((1, D_MODEL)),
                  _const_spec((D_MODEL, 2 * D_FF)),
                  _const_spec((D_FF, D_MODEL))] + side_specs,
        out_specs=[pl.BlockSpec((tm, D_MODEL), lambda i: (i, 0))] + side_specs,
        out_shape=[jax.ShapeDtypeStruct(x2.shape, _F32)] + [jax.ShapeDtypeStruct(w.shape, _BF16) for w in flat],
        compiler_params=_params("parallel"),
        name="ffn_dense",
    )(x2, g, w_gu, w_down, *flat)
    return outs[0], [o.reshape(w.shape) for o, w in zip(outs[1:], side_casts)]


def _proj_b_kernel(x_ref, g_ref, w_ref, o_ref):
    h = _rms(x_ref[...], g_ref[...]).astype(_BF16)
    proj = _dot(h, w_ref[...])
    col = lax.broadcasted_iota(jnp.int32, (1, proj.shape[1]), 1)
    is_query = (col < TOK_WIDTH) | (col >= 3 * TOK_WIDTH)
    o_ref[...] = (proj * jnp.where(is_query, ATTN_SCALE, 1.0)).astype(_BF16)


def _proj_b(x2, g, w_in, *, tm):
    t = x2.shape[0]
    n = w_in.shape[1]
    return pl.pallas_call(
        _proj_b_kernel,
        grid=(t // tm,),
        in_specs=[pl.BlockSpec((tm, D_MODEL), lambda i: (i, 0)),
                  _const_spec((1, D_MODEL)),
                  _const_spec((D_MODEL, n))],
        out_specs=pl.BlockSpec((tm, n), lambda i: (i, 0)),
        out_shape=jax.ShapeDtypeStruct((t, n), _BF16),
        compiler_params=_params("parallel"),
        name="proj_natten",
    )(x2, g, w_in)


def _natten_pattern(step, n_steps):
    return jnp.where(step < 2, step, jnp.where(step >= n_steps - 2, step - n_steps + NAT_PATTERNS, 2))


def _natten_window_start(step, rows):
    first = jnp.clip(NAT_ROWS_PER_STEP * step - WIN_H // 2, 0, rows - WIN_H)
    return jnp.minimum(first, rows - NAT_WIN_ROWS)


def _natten_kernel(q_ref, k_ref, v_ref, tab_ref, *rest, rows):
    o_ref = rest[-1]
    n_steps = rows // NAT_ROWS_PER_STEP
    nq = NAT_ROWS_PER_STEP * GRID_W
    low_half = lax.broadcasted_iota(jnp.int32, (1, LANES), 1) < HEAD_DIM

    ones = jnp.ones((NAT_KEYS, LANES), _BF16)

    def one_step(step):
        key0 = pl.multiple_of(_natten_window_start(step, rows) * GRID_W, GRID_W)
        q0 = pl.multiple_of(step * nq, nq)
        kw = k_ref[0, pl.ds(key0, NAT_KEYS), :]
        vw = v_ref[0, pl.ds(key0, NAT_KEYS), :]
        q2 = q_ref[0, pl.ds(q0, nq), :]
        zero = jnp.zeros_like(q2)
        qm = jnp.concatenate([jnp.where(low_half, q2, zero), jnp.where(low_half, zero, q2)], axis=0)
        st = lax.dot_general(kw, qm, _NT, preferred_element_type=_F32)
        st = st + tab_ref[0, _natten_pattern(step, n_steps)]
        e = jnp.exp(st - jnp.max(st, axis=0, keepdims=True)).astype(_BF16)
        o2 = lax.dot_general(e, jnp.concatenate([vw, ones], axis=1), _TN,
                             preferred_element_type=_F32)
        o2 = o2[:, :LANES] / o2[:, LANES:]
        o_ref[0, pl.ds(q0, nq), :] = jnp.where(low_half, o2[:nq], o2[nq:]).astype(_BF16)

    def trip(i, carry):
        for u in range(NAT_UNROLL):
            one_step(i * NAT_UNROLL + u)
        return carry

    assert n_steps % NAT_UNROLL == 0
    lax.fori_loop(0, n_steps // NAT_UNROLL, trip, 0)


def _natten_bias_kernel(picked_ref, expand_ref, o_ref):
    o_ref[0] = jnp.dot(picked_ref[0], expand_ref[...], preferred_element_type=_F32,
                       precision=lax.Precision.HIGHEST)


def _natten_bias_table(rpb, rows):
    n_steps = rows // NAT_ROWS_PER_STEP
    assert rows % NAT_ROWS_PER_STEP == 0 and rows >= NAT_WIN_ROWS and n_steps >= NAT_PATTERNS
    steps = np.array([0, 1, 2, n_steps - 2, n_steps - 1])
    first = np.clip(NAT_ROWS_PER_STEP * steps - WIN_H // 2, 0, rows - WIN_H)
    win0 = np.minimum(first, rows - NAT_WIN_ROWS)
    i = np.arange(NAT_WIN_ROWS)[None, :, None, None, None]
    kc = np.arange(GRID_W)[None, None, :, None, None]
    j = np.arange(NAT_ROWS_PER_STEP)[None, None, None, :, None]
    qc = np.arange(GRID_W)[None, None, None, None, :]
    r = NAT_ROWS_PER_STEP * steps[:, None, None, None, None] + j
    key_row = win0[:, None, None, None, None] + i
    start = np.clip(r - WIN_H // 2, 0, rows - WIN_H)
    row_ok = (key_row >= start) & (key_row < start + WIN_H)
    col_start = np.clip(qc - WIN_W // 2, 0, GRID_W - WIN_W)
    col_ok = (kc >= col_start) & (kc < col_start + WIN_W)
    rel_row = np.clip(key_row - r + (WIN_H - 1), 0, 2 * WIN_H - 2)
    rel_col = np.clip(kc - qc + (WIN_W - 1), 0, 2 * WIN_W - 2)
    shape = (NAT_PATTERNS, NAT_WIN_ROWS, GRID_W, NAT_ROWS_PER_STEP, GRID_W)
    ok = np.broadcast_to(row_ok & col_ok, shape)
    n_heads, n_rel_col = rpb.shape[0], rpb.shape[2]
    row_sel = rel_row.reshape(-1)
    n_sel = -(-row_sel.size // SUBLANES) * SUBLANES
    picked = jnp.zeros((n_heads, n_sel, LANES), _F32)
    picked = picked.at[:, :row_sel.size, :n_rel_col].set(rpb.astype(_F32)[:, row_sel, :])
    expand = np.zeros((LANES, GRID_W * GRID_W), np.float32)
    expand[rel_col.reshape(-1), np.arange(GRID_W * GRID_W)] = 1.0
    tab = pl.pallas_call(
        _natten_bias_kernel,
        grid=(n_heads,),
        in_specs=[pl.BlockSpec((1, n_sel, LANES), lambda h: (h, 0, 0)),
                  _const_spec(expand.shape)],
        out_specs=pl.BlockSpec((1, n_sel, GRID_W * GRID_W), lambda h: (h, 0, 0)),
        out_shape=jax.ShapeDtypeStruct((n_heads, n_sel, GRID_W * GRID_W), _F32),
        compiler_params=_params("parallel"),
        name="natten_bias",
    )(picked, jnp.asarray(expand))
    nq = NAT_ROWS_PER_STEP * GRID_W
    tab = tab[:, :row_sel.size].reshape(N_PAIR, 2, NAT_PATTERNS, NAT_WIN_ROWS, NAT_ROWS_PER_STEP, GRID_W, GRID_W)
    tab = tab.transpose(0, 2, 3, 5, 1, 4, 6)
    tab = jnp.where(ok[None, :, :, :, None], tab, NEG_INF)
    return tab.reshape(N_PAIR, NAT_PATTERNS, NAT_KEYS, 2 * nq)


def _natten(proj3, table, run_after=None):
    b, s, _ = proj3.shape
    rows = s // GRID_W
    blk = (1, s, LANES)
    order_specs = [] if run_after is None else [pl.BlockSpec((SUBLANES, LANES), lambda p, i: (0, 0))]
    order_args = [] if run_after is None else [run_after]
    return pl.pallas_call(
        functools.partial(_natten_kernel, rows=rows),
        grid=(N_PAIR, b),
        in_specs=[pl.BlockSpec(blk, lambda p, i: (i, 0, p)),
                  pl.BlockSpec(blk, lambda p, i: (i, 0, N_PAIR + p)),
                  pl.BlockSpec(blk, lambda p, i: (i, 0, 2 * N_PAIR + p)),
                  pl.BlockSpec((1,) + table.shape[1:], lambda p, i: (p, 0, 0, 0))] + order_specs,
        out_specs=pl.BlockSpec(blk, lambda p, i: (i, 0, p)),
        out_shape=jax.ShapeDtypeStruct((b, s, TOK_WIDTH), _BF16),
        compiler_params=_params("parallel", "parallel"),
        name="natten",
    )(proj3, proj3, proj3, table, *order_args)


def _mixer_b_out_kernel(x_ref, tok_ref, qx_ref, k_ref, v_ref, wout_ref, g_ref, wr_ref,
                        o_ref, h_ref, idx_ref, gate_ref):
    x = _xattn_outproj(x_ref[0], tok_ref[0], qx_ref[0], k_ref[0], v_ref[0], wout_ref)
    o_ref[0] = x
    h = _rms(x, g_ref[...])
    h_ref[...] = _pack_halves(h)
    idx_ref[...], gate_ref[...] = _route(h, wr_ref)


def _mixer_b_out(x, tok, proj3, k, v, w_out, g_ffn, w_router2, *, tm):
    b, s, _ = x.shape
    t = b * s
    nj = s // tm
    qx_block = 3 * TOK_WIDTH // XA_WIDTH
    return pl.pallas_call(
        _mixer_b_out_kernel,
        grid=(b, nj),
        in_specs=[pl.BlockSpec((1, tm, D_MODEL), lambda i, j: (i, j, 0)),
                  pl.BlockSpec((1, tm, TOK_WIDTH), lambda i, j: (i, j, 0)),
                  pl.BlockSpec((1, tm, XA_WIDTH), lambda i, j: (i, j, qx_block)),
                  pl.BlockSpec((1, N_MEM, XA_WIDTH), lambda i, j: (i, 0, 0)),
                  pl.BlockSpec((1, N_MEM, XA_WIDTH), lambda i, j: (i, 0, 0)),
                  _const_spec((D_MODEL, D_MODEL)),
                  _const_spec((1, D_MODEL)),
                  _const_spec((D_MODEL, 2 * LANES))],
        out_specs=[pl.BlockSpec((1, tm, D_MODEL), lambda i, j: (i, j, 0)),
                   pl.BlockSpec((tm, D_MODEL // 2), lambda i, j: (i * nj + j, 0)),
                   pl.BlockSpec((SUBLANES, tm), lambda i, j: (0, i * nj + j)),
                   pl.BlockSpec((SUBLANES, tm), lambda i, j: (0, i * nj + j))],
        out_shape=[jax.ShapeDtypeStruct(x.shape, _F32),
                   jax.ShapeDtypeStruct((t, D_MODEL // 2), jnp.uint32),
                   jax.ShapeDtypeStruct((SUBLANES, t), jnp.int32),
                   jax.ShapeDtypeStruct((SUBLANES, t), _F32)],
        compiler_params=_params("parallel", "parallel"),
        name="mixer_natten_out",
    )(x, tok, proj3, k, v, w_out, g_ffn, w_router2)


def _pack_halves(h):
    half = D_MODEL // 2
    hi = pltpu.bitcast(h[:, :half].astype(_BF16).astype(_F32), jnp.uint32)
    lo = pltpu.bitcast(h[:, half:].astype(_BF16).astype(_F32), jnp.uint32)
    return hi | (lo >> 16)


def _unpack_halves(w):
    hi = pltpu.bitcast(w & jnp.uint32(0xFFFF0000), _F32).astype(_BF16)
    lo = pltpu.bitcast(w << 16, _F32).astype(_BF16)
    return jnp.concatenate([hi, lo], axis=1)


def _route(h, wr_ref):
    h_head = h.astype(_BF16)
    h_rest = (h - h_head.astype(_F32)).astype(_BF16)
    both = _dot(h_head, wr_ref[...])
    logits = both[:, :LANES] + both[:, LANES:] + _dot(h_rest, wr_ref[:, :LANES])
    assert N_EXPERTS == SUBLANES
    lt = jnp.concatenate([jnp.transpose(logits[c * LANES:(c + 1) * LANES])[:SUBLANES]
                          for c in range(h.shape[0] // LANES)], axis=1)
    expert = lax.broadcasted_iota(jnp.int32, lt.shape, 0)
    v1 = jnp.max(lt, axis=0, keepdims=True)
    i1 = jnp.min(jnp.where(lt == v1, expert, N_EXPERTS), axis=0, keepdims=True)
    lt2 = jnp.where(expert == i1, -jnp.inf, lt)
    v2 = jnp.max(lt2, axis=0, keepdims=True)
    i2 = jnp.min(jnp.where(lt2 == v2, expert, N_EXPERTS), axis=0, keepdims=True)
    e2 = jnp.exp(v2 - v1)
    g1 = 1.0 / (1.0 + e2)
    g2 = e2 / (1.0 + e2)
    idx = jnp.where(expert == 0, i1, jnp.where(expert == 1, i2, 0))
    gate = jnp.where(expert == 0, g1, jnp.where(expert == 1, g2, 0.0))
    return idx, gate


def _gather_rows(table, idx):
    n, d = idx.shape[0], table.shape[1]
    workers = SC_CORES * SC_SUBCORES
    per_worker = n // workers
    pairs = per_worker // (2 * SC_WINDOW)
    assert n == workers * pairs * 2 * SC_WINDOW, (n, workers, SC_WINDOW)
    mesh = plsc.VectorSubcoreMesh(core_axis_name="core", subcore_axis_name="subcore",
                                  num_cores=SC_CORES, num_subcores=SC_SUBCORES)

    def gather_kernel(table_hbm, idx_hbm, out_hbm, idx_a, idx_b, rows_a, rows_b, sem_a, sem_b):
        worker = lax.axis_index("subcore") * SC_CORES + lax.axis_index("core")
        base = worker * per_worker

        @pl.loop(0, pairs)
        def _(p):
            r0 = base + p * (2 * SC_WINDOW)
            r1 = r0 + SC_WINDOW
            pltpu.sync_copy(idx_hbm.at[pl.ds(r0, SC_WINDOW)], idx_a)
            gather_a = pltpu.async_copy(table_hbm.at[idx_a], rows_a, sem_a)
            pltpu.sync_copy(idx_hbm.at[pl.ds(r1, SC_WINDOW)], idx_b)
            gather_b = pltpu.async_copy(table_hbm.at[idx_b], rows_b, sem_b)
            gather_a.wait()
            pltpu.sync_copy(rows_a, out_hbm.at[pl.ds(r0, SC_WINDOW)])
            gather_b.wait()
            pltpu.sync_copy(rows_b, out_hbm.at[pl.ds(r1, SC_WINDOW)])

    idx_buf = pltpu.VMEM((SC_WINDOW,), jnp.int32)
    row_buf = pltpu.VMEM((SC_WINDOW, d), table.dtype)
    return pl.kernel(gather_kernel, out_type=jax.ShapeDtypeStruct((n, d), table.dtype), mesh=mesh,
                     scratch_types=[idx_buf, idx_buf, row_buf, row_buf,
                                    pltpu.SemaphoreType.DMA, pltpu.SemaphoreType.DMA],
                     name="moe_row_gather")(table, idx)


def _expert_kernel(meta_ref, xs_ref, wgu_ref, wd_ref, ys_ref, *, n_blocks):
    @pl.when(pl.program_id(0) < meta_ref[n_blocks])
    def _():
        h = _unpack_halves(xs_ref[...])
        acc = jnp.zeros((MOE_ROWS, D_MODEL), _F32)
        for c in range(D_FF_EXPERT // FF_CHUNK_EXPERT):
            lo = c * FF_CHUNK_EXPERT
            gate = _dot(h, wgu_ref[0, :, lo:lo + FF_CHUNK_EXPERT])
            up = _dot(h, wgu_ref[0, :, D_FF_EXPERT + lo:D_FF_EXPERT + lo + FF_CHUNK_EXPERT])
            a = (_silu(gate) * up).astype(_BF16)
            acc = acc + _dot(a, wd_ref[0, lo:lo + FF_CHUNK_EXPERT, :])
        ys_ref[...] = _pack_halves(acc)


def _experts(meta, xs, w_gu, w_down):
    n_rows, half = xs.shape
    n_blocks = n_rows // MOE_ROWS
    grid_spec = pltpu.PrefetchScalarGridSpec(
        num_scalar_prefetch=1,
        grid=(n_blocks,),
        in_specs=[pl.BlockSpec((MOE_ROWS, half), lambda i, m: (i, 0)),
                  pl.BlockSpec((1, D_MODEL, 2 * D_FF_EXPERT), lambda i, m: (m[i], 0, 0),
                               pipeline_mode=pl.Buffered(1)),
                  pl.BlockSpec((1, D_FF_EXPERT, D_MODEL), lambda i, m: (m[i], 0, 0),
                               pipeline_mode=pl.Buffered(1))],
        out_specs=pl.BlockSpec((MOE_ROWS, half), lambda i, m: (i, 0)),
    )
    return pl.pallas_call(
        functools.partial(_expert_kernel, n_blocks=n_blocks),
        grid_spec=grid_spec,
        out_shape=jax.ShapeDtypeStruct((n_rows, half), jnp.uint32),
        compiler_params=_params("arbitrary"),
        name="moe_experts",
    )(meta, xs, w_gu, w_down)


def _combine_kernel(x_ref, y0_ref, y1_ref, gate_ref, g_ref, o_ref, *, tc):
    gates = gate_ref[...]
    pad = jnp.zeros((LANES - SUBLANES, LANES), _F32)
    cols = [jnp.transpose(jnp.concatenate([gates[:, c * LANES:(c + 1) * LANES], pad], axis=0))
            for c in range(tc // LANES)]
    gcol = jnp.concatenate(cols, axis=0)
    y0 = _unpack_halves(y0_ref[...]).astype(_F32)
    y1 = _unpack_halves(y1_ref[...]).astype(_F32)
    f = y0 * gcol[:, 0:1] + y1 * gcol[:, 1:2]
    o_ref[...] = _rms(x_ref[...] + f, g_ref[...])


def _combine(x2, y0, y1, gates, g_final, *, tc):
    t = x2.shape[0]
    half = y0.shape[1]
    return pl.pallas_call(
        functools.partial(_combine_kernel, tc=tc),
        grid=(t // tc,),
        in_specs=[pl.BlockSpec((tc, D_MODEL), lambda i: (i, 0)),
                  pl.BlockSpec((tc, half), lambda i: (i, 0)),
                  pl.BlockSpec((tc, half), lambda i: (i, 0)),
                  pl.BlockSpec((SUBLANES, tc), lambda i: (0, i)),
                  _const_spec((1, D_MODEL))],
        out_specs=pl.BlockSpec((tc, D_MODEL), lambda i: (i, 0)),
        out_shape=jax.ShapeDtypeStruct(x2.shape, _F32),
        compiler_params=_params("parallel"),
        name="moe_combine",
    )(x2, y0, y1, gates, g_final)


def _routing_offsets(expert_idx, n_blocks):
    t = expert_idx.shape[1]
    n = 2 * t
    flat_e = expert_idx[:2].reshape(n)
    experts = jnp.arange(N_EXPERTS, dtype=jnp.int32)
    onehot = (flat_e[:, None] == experts[None, :]).astype(jnp.int32)
    running = jnp.cumsum(onehot, axis=0)
    rank = jnp.sum((running - onehot) * onehot, axis=1)
    counts = running[-1]
    padded = (counts + MOE_ROWS - 1) // MOE_ROWS * MOE_ROWS
    pad_end = jnp.cumsum(padded)
    pad_start = pad_end - padded
    start = jnp.cumsum(counts) - counts
    dest = (jnp.sum(onehot * pad_start[None, :], axis=1) + rank).astype(jnp.int32)

    block_row = jnp.arange(n_blocks, dtype=jnp.int32) * MOE_ROWS
    block_expert = jnp.minimum(jnp.sum((block_row[:, None] >= pad_end[None, :]).astype(jnp.int32), axis=1),
                               N_EXPERTS - 1)
    meta = jnp.concatenate([block_expert, pad_end[-1:] // MOE_ROWS]).astype(jnp.int32)

    assert N_EXPERTS * n < 2 ** 31
    order = jnp.sort(flat_e * n + jnp.arange(n, dtype=jnp.int32)) % n
    row_onehot = jnp.repeat((block_expert[:, None] == experts[None, :]).astype(jnp.int32), MOE_ROWS, axis=0)
    off = jnp.arange(n_blocks * MOE_ROWS, dtype=jnp.int32) - jnp.sum(row_onehot * pad_start[None, :], axis=1)
    valid = off < jnp.sum(row_onehot * counts[None, :], axis=1)
    pos = jnp.clip(jnp.sum(row_onehot * start[None, :], axis=1) + off, 0, n - 1)
    src_tok = jnp.where(valid, order[pos] % t, 0).astype(jnp.int32)
    return dest, src_tok, meta


def _dense_layers(x, mem, w, w_experts=None, run_after=None):
    b, s, d = x.shape
    t = b * s

    k0, v0, k1, v1 = _mem_kv(mem, w['norm_mem'], w['w_mem_kv'])
    x = _mixer_a(x, k0, v0, w['norm_mix'][0], w['w_in_a'], w['g_v_a'], w['w_s_a'], w['b_s_a'],
                 w['w_out'][0], tm=ROW_TILE)
    side = () if w_experts is not None else (w['w_exp_gu'], w['w_exp_down'])
    x2, cast = _ffn(x.reshape(t, d), w['norm_ffn'][0], w['w_ffn_gu'], w['w_ffn_down'], side, tm=ROW_TILE)
    w_exp_gu, w_exp_down = w_experts if w_experts is not None else cast

    proj = _proj_b(x2, w['norm_mix'][1], w['w_in_b'], tm=ROW_TILE).reshape(b, s, -1)
    tok = _natten(proj, _natten_bias_table(w['rpb_b'], s // GRID_W), run_after)
    x, h_packed, expert_idx, gates = _mixer_b_out(x2.reshape(b, s, d), tok, proj, k1, v1, w['w_out'][1],
                                                  w['norm_ffn'][1], w['w_router'], tm=ROW_TILE)
    return x.reshape(t, d), h_packed, expert_idx, gates, (w_exp_gu, w_exp_down)


def _dispatch(h_packed, expert_idx):
    t = h_packed.shape[0]
    n_blocks = 2 * t // MOE_ROWS + N_EXPERTS
    dest, src_tok, meta = _routing_offsets(expert_idx, n_blocks)
    return _gather_rows(h_packed, src_tok), dest, src_tok, meta


def _experts_combine(x2, gates, xs, dest, meta, w_experts, g_final):
    t = x2.shape[0]
    ys = _experts(meta, xs, *w_experts)
    y0 = _gather_rows(ys, dest[:t])
    y1 = _gather_rows(ys, dest[t:])
    return _combine(x2, y0, y1, gates, g_final, tc=ROW_TILE)


def _cast_kernel(w_ref, o_ref):
    o_ref[...] = w_ref[...].astype(o_ref.dtype)


def _to_bf16(w, rows_per_step):
    n, r, c = w.shape
    assert r % rows_per_step == 0
    block = (1, rows_per_step, c)
    return pl.pallas_call(
        _cast_kernel,
        grid=(n, r // rows_per_step),
        in_specs=[pl.BlockSpec(block, lambda i, j: (i, j, 0))],
        out_specs=pl.BlockSpec(block, lambda i, j: (i, j, 0)),
        out_shape=jax.ShapeDtypeStruct(w.shape, _BF16),
        compiler_params=_params("parallel", "parallel"),
        name="weight_cast",
    )(w)


def _split_router_weight(w):
    head = w.astype(_BF16)
    rest = (w - head.astype(_F32)).astype(_BF16)
    pad = ((0, 0), (0, LANES - N_EXPERTS))
    return jnp.concatenate([jnp.pad(head, pad), jnp.pad(rest, pad)], axis=1)


def _prepare_weights(norm_mix, norm_mem, norm_ffn, norm_final, w_in_a, g_v_a, w_s_a, b_s_a, w_in_b, rpb_b,
                     w_mem_kv, w_out, w_ffn_gu, w_ffn_down, w_router, w_exp_gu, w_exp_down):
    assert w_in_a.shape[0] == 1 and w_in_b.shape[0] == 1 and norm_mix.shape[0] == 2
    row = lambda a: a.astype(_F32)[:, None, :]
    return {
        'norm_mix': row(norm_mix), 'norm_mem': row(norm_mem), 'norm_ffn': row(norm_ffn),
        'norm_final': norm_final.astype(_F32)[None, :],
        'w_in_a': w_in_a[0].astype(_BF16),
        'g_v_a': g_v_a.astype(_F32),
        'w_s_a': jnp.concatenate([w_s_a[0, 0::2], w_s_a[0, 1::2]], axis=2).astype(_BF16),
        'b_s_a': jnp.repeat(b_s_a[0].astype(_F32).T, HEAD_DIM, axis=1),
        'w_in_b': w_in_b[0].astype(_BF16),
        'rpb_b': rpb_b[0],
        'w_mem_kv': w_mem_kv.astype(_BF16),
        'w_out': w_out.astype(_BF16),
        'w_ffn_gu': _to_bf16(w_ffn_gu, 256)[0],
        'w_ffn_down': _to_bf16(w_ffn_down, 704)[0],
        'w_router': _split_router_weight(w_router[0].astype(_F32)),
        'w_exp_gu': w_exp_gu[0].astype(_F32),
        'w_exp_down': w_exp_down[0].astype(_F32),
    }


def kernel(x_prompt, x_sample, mem_prompt, mem_sample, norm_mix, norm_mem, norm_ffn, norm_final, w_in_a, g_v_a, w_s_a, b_s_a, w_in_b, rpb_b, w_mem_kv, w_out, w_ffn_gu, w_ffn_down, w_router, w_exp_gu, w_exp_down):
    w = _prepare_weights(norm_mix, norm_mem, norm_ffn, norm_final, w_in_a, g_v_a, w_s_a, b_s_a, w_in_b, rpb_b,
                         w_mem_kv, w_out, w_ffn_gu, w_ffn_down, w_router, w_exp_gu, w_exp_down)
    x2_p, h_p, idx_p, gates_p, w_experts = _dense_layers(x_prompt, mem_prompt, w)
    xs_p, dest_p, _, meta_p = _dispatch(h_p, idx_p)
    x2_s, h_s, idx_s, gates_s, _ = _dense_layers(x_sample, mem_sample, w, w_experts, run_after=xs_p)
    xs_s, dest_s, _, meta_s = _dispatch(h_s, idx_s)
    y_prompt = _experts_combine(x2_p, gates_p, xs_p, dest_p, meta_p, w_experts, w['norm_final'])
    y_sample = _experts_combine(x2_s, gates_s, xs_s, dest_s, meta_s, w_experts, w['norm_final'])
    return (y_prompt.reshape(x_prompt.shape), y_sample.reshape(x_sample.shape))
```

```python
import functools

import numpy as np
import jax
import jax.numpy as jnp
from jax import lax
from jax.experimental import pallas as pl
from jax.experimental.pallas import tpu as pltpu
from jax.experimental.pallas import tpu_sc as plsc

D_MODEL = 1024
HEAD_DIM = 64
XA_HEADS = 4
XA_WIDTH = XA_HEADS * HEAD_DIM
TOK_WIDTH = D_MODEL - XA_WIDTH
CHUNK = 128
N_PAIR = TOK_WIDTH // (2 * HEAD_DIM)
GRID_W = 64
WIN_H = 8
WIN_W = 16
N_MEM = 256
D_FF = 2816
N_EXPERTS = 8
D_FF_EXPERT = 3584
RMS_EPS = 1e-6
NEG_INF = -1e30
ATTN_SCALE = HEAD_DIM ** -0.5

LANES = 128
SUBLANES = 8
VMEM_LIMIT = 56 * 1024 * 1024

NAT_ROWS_PER_STEP = 2
NAT_WIN_ROWS = WIN_H + NAT_ROWS_PER_STEP - 1
NAT_KEYS = NAT_WIN_ROWS * GRID_W
NAT_PATTERNS = 5
NAT_UNROLL = 4

MOE_ROWS = 512
SC_CORES = 2
SC_SUBCORES = 16
SC_WINDOW = 64
MXU_WIDTH = 256
FF_SPLIT_DENSE = 6 * MXU_WIDTH
ROW_TILE = 512
MIXER_TILE = 1024
FF_CHUNK_EXPERT = 512

_BF16 = jnp.bfloat16
_F32 = jnp.float32
_NT = (((1,), (1,)), ((), ()))
_TN = (((0,), (0,)), ((), ()))


def _params(*semantics):
    return pltpu.CompilerParams(dimension_semantics=semantics, vmem_limit_bytes=VMEM_LIMIT)


def _const_spec(shape):
    zeros = (0,) * len(shape)
    return pl.BlockSpec(shape, lambda *_: zeros, pipeline_mode=pl.Buffered(1))


def _rms(x, g):
    ms = jnp.mean(x * x, axis=-1, keepdims=True)
    return x * lax.rsqrt(ms + RMS_EPS) * g


def _dot(a, b):
    return jnp.dot(a, b, preferred_element_type=_F32)


def _gelu(x):
    return 0.5 * x * (1.0 + lax.erf(x * np.float32(1.0 / np.sqrt(2.0))))


def _silu(x):
    return x / (1.0 + jnp.exp(-x))


def _softmax_rows(s):
    m = jnp.max(s, axis=-1, keepdims=True)
    e = jnp.exp(s - m)
    return e / jnp.sum(e, axis=-1, keepdims=True)


def _xattn_outproj(x, tok, qx, k, v, wout_ref):
    lane = lax.broadcasted_iota(jnp.int32, (1, XA_WIDTH), 1)
    qx = qx.astype(_F32)
    xa = jnp.zeros(qx.shape, _F32)
    for h in range(XA_HEADS):
        in_head = (lane >= h * HEAD_DIM) & (lane < (h + 1) * HEAD_DIM)
        qh = jnp.where(in_head, qx, 0.0).astype(_BF16)
        s = lax.dot_general(qh, k, _NT, preferred_element_type=_F32)
        p = _softmax_rows(s).astype(_BF16)
        xa = jnp.where(in_head, _dot(p, v), xa)
    y = _dot(tok, wout_ref[:TOK_WIDTH, :]) + _dot(xa.astype(_BF16), wout_ref[TOK_WIDTH:, :])
    return x + y


def _mem_kv_kernel(mem_ref, g_ref, w_ref, *kv_refs):
    mem = mem_ref[0]
    for layer in range(w_ref.shape[0]):
        m = _rms(mem, g_ref[layer]).astype(_BF16)
        kv = _dot(m, w_ref[layer])
        kv_refs[2 * layer][0] = kv[:, :XA_WIDTH].astype(_BF16)
        kv_refs[2 * layer + 1][0] = kv[:, XA_WIDTH:].astype(_BF16)


def _mem_kv(mem, g, w_kv):
    b = mem.shape[0]
    layers = w_kv.shape[0]
    out = jax.ShapeDtypeStruct((b, N_MEM, XA_WIDTH), _BF16)
    return pl.pallas_call(
        _mem_kv_kernel,
        grid=(b,),
        in_specs=[pl.BlockSpec((1, N_MEM, D_MODEL), lambda i: (i, 0, 0)),
                  _const_spec((layers, 1, D_MODEL)),
                  _const_spec((layers, D_MODEL, 2 * XA_WIDTH))],
        out_specs=[pl.BlockSpec((1, N_MEM, XA_WIDTH), lambda i: (i, 0, 0))] * (2 * layers),
        out_shape=[out] * (2 * layers),
        compiler_params=_params("parallel"),
        name="mem_kv",
    )(mem, g, w_kv)


def _mixer_a_kernel(x_ref, k_ref, v_ref, g_ref, win_ref, gv_ref, ws_ref, bs_ref, wout_ref, o_ref, *, tm):
    assert tm % (2 * CHUNK) == 0
    x = x_ref[0]
    h = _rms(x, g_ref[...]).astype(_BF16)
    proj = _dot(h, win_ref[...])
    z = _gelu(proj[:, :2 * TOK_WIDTH])
    u = z[:, :TOK_WIDTH]
    vn = _rms(z[:, TOK_WIDTH:], gv_ref[...]).astype(_BF16)
    low_half = lax.broadcasted_iota(jnp.int32, (1, LANES), 1) < HEAD_DIM
    bias = bs_ref[...]
    zero = jnp.zeros((CHUNK, LANES), _BF16)
    toks = []
    for c in range(0, tm // CHUNK, 2):
        va, vb = vn[c * CHUNK:(c + 1) * CHUNK], vn[(c + 1) * CHUNK:(c + 2) * CHUNK]
        cols_a, cols_b = [], []
        for j in range(N_PAIR):
            pa, pb = va[:, j * LANES:(j + 1) * LANES], vb[:, j * LANES:(j + 1) * LANES]
            even = jnp.concatenate([jnp.where(low_half, pa, zero), jnp.where(low_half, pb, zero)], axis=1)
            odd = jnp.concatenate([jnp.where(low_half, zero, pa), jnp.where(low_half, zero, pb)], axis=1)
            r = _dot(ws_ref[j], jnp.concatenate([even, odd], axis=0))
            cols_a.append(r[:, :LANES])
            cols_b.append(r[:, LANES:])
        for side, cols in enumerate((cols_a, cols_b)):
            sv = jnp.concatenate(cols, axis=1) + bias
            toks.append(u[(c + side) * CHUNK:(c + side + 1) * CHUNK] * sv)
    tok = jnp.concatenate(toks, axis=0).astype(_BF16)
    qx = proj[:, 2 * TOK_WIDTH:] * ATTN_SCALE
    o_ref[0] = _xattn_outproj(x, tok, qx, k_ref[0], v_ref[0], wout_ref)


def _mixer_a(x, k, v, g, w_in, g_v, w_s2, b_s2, w_out, *, tm):
    b, s, _ = x.shape
    return pl.pallas_call(
        functools.partial(_mixer_a_kernel, tm=tm),
        grid=(b, s // tm),
        in_specs=[pl.BlockSpec((1, tm, D_MODEL), lambda i, j: (i, j, 0)),
                  pl.BlockSpec((1, N_MEM, XA_WIDTH), lambda i, j: (i, 0, 0)),
                  pl.BlockSpec((1, N_MEM, XA_WIDTH), lambda i, j: (i, 0, 0)),
                  _const_spec((1, D_MODEL)),
                  _const_spec((D_MODEL, 2 * TOK_WIDTH + XA_WIDTH)),
                  _const_spec((1, TOK_WIDTH)),
                  _const_spec((N_PAIR, CHUNK, 2 * CHUNK)),
                  _const_spec((CHUNK, TOK_WIDTH)),
                  _const_spec((D_MODEL, D_MODEL))],
        out_specs=pl.BlockSpec((1, tm, D_MODEL), lambda i, j: (i, j, 0)),
        out_shape=jax.ShapeDtypeStruct(x.shape, _F32),
        compiler_params=_params("parallel", "parallel"),
        name="mixer_gmlp",
    )(x, k, v, g, w_in, g_v, w_s2, b_s2, w_out)


def _ffn_kernel(x_ref, g_ref, wgu_ref, wd_ref, *refs):
    n_side = (len(refs) - 1) // 2
    o_ref = refs[n_side]
    x = x_ref[...]
    h = _rms(x, g_ref[...]).astype(_BF16)
    acc = x
    for lo, hi in ((0, FF_SPLIT_DENSE), (FF_SPLIT_DENSE, D_FF)):
        gate = _dot(h, wgu_ref[:, lo:hi])
        up = _dot(h, wgu_ref[:, D_FF + lo:D_FF + hi])
        a = (_silu(gate) * up).astype(_BF16)
        acc = acc + _dot(a, wd_ref[lo:hi, :])
    o_ref[...] = acc
    for src, dst in zip(refs[:n_side], refs[n_side + 1:]):
        dst[...] = src[...].astype(dst.dtype)


def _ffn(x2, g, w_gu, w_down, side_casts=(), *, tm):
    t = x2.shape[0]
    steps = t // tm
    flat = [w.reshape(-1, w.shape[-1]) for w in side_casts]
    slabs = [(w.shape[0] // steps, w.shape[1]) for w in flat]
    assert all(w.shape[0] == s[0] * steps and s[0] % (2 * SUBLANES) == 0 for w, s in zip(flat, slabs))
    side_specs = [pl.BlockSpec(s, lambda i: (i, 0)) for s in slabs]
    outs = pl.pallas_call(
        _ffn_kernel,
        grid=(steps,),
        in_specs=[pl.BlockSpec((tm, D_MODEL), lambda i: (i, 0)),
                  _const_spec((1, D_MODEL)),
                  _const_spec((D_MODEL, 2 * D_FF)),
                  _const_spec((D_FF, D_MODEL))] + side_specs,
        out_specs=[pl.BlockSpec((tm, D_MODEL), lambda i: (i, 0))] + side_specs,
        out_shape=[jax.ShapeDtypeStruct(x2.shape, _F32)] + [jax.ShapeDtypeStruct(w.shape, _BF16) for w in flat],
        compiler_params=_params("parallel"),
        name="ffn_dense",
    )(x2, g, w_gu, w_down, *flat)
    return outs[0], [o.reshape(w.shape) for o, w in zip(outs[1:], side_casts)]


def _proj_b_kernel(x_ref, g_ref, w_ref, o_ref):
    h = _rms(x_ref[...], g_ref[...]).astype(_BF16)
    proj = _dot(h, w_ref[...])
    col = lax.broadcasted_iota(jnp.int32, (1, proj.shape[1]), 1)
    is_query = (col < TOK_WIDTH) | (col >= 3 * TOK_WIDTH)
    o_ref[...] = (proj * jnp.where(is_query, ATTN_SCALE, 1.0)).astype(_BF16)


def _proj_b(x2, g, w_in, *, tm):
    t = x2.shape[0]
    n = w_in.shape[1]
    return pl.pallas_call(
        _proj_b_kernel,
        grid=(t // tm,),
        in_specs=[pl.BlockSpec((tm, D_MODEL), lambda i: (i, 0)),
                  _const_spec((1, D_MODEL)),
                  _const_spec((D_MODEL, n))],
        out_specs=pl.BlockSpec((tm, n), lambda i: (i, 0)),
        out_shape=jax.ShapeDtypeStruct((t, n), _BF16),
        compiler_params=_params("parallel"),
        name="proj_natten",
    )(x2, g, w_in)


def _natten_pattern(step, n_steps):
    return jnp.where(step < 2, step, jnp.where(step >= n_steps - 2, step - n_steps + NAT_PATTERNS, 2))


def _natten_window_start(step, rows):
    first = jnp.clip(NAT_ROWS_PER_STEP * step - WIN_H // 2, 0, rows - WIN_H)
    return jnp.minimum(first, rows - NAT_WIN_ROWS)


def _natten_kernel(q_ref, k_ref, v_ref, tab_ref, *rest, rows):
    o_ref = rest[-1]
    n_steps = rows // NAT_ROWS_PER_STEP
    nq = NAT_ROWS_PER_STEP * GRID_W
    low_half = lax.broadcasted_iota(jnp.int32, (1, LANES), 1) < HEAD_DIM

    ones = jnp.ones((NAT_KEYS, LANES), _BF16)

    def one_step(step):
        key0 = pl.multiple_of(_natten_window_start(step, rows) * GRID_W, GRID_W)
        q0 = pl.multiple_of(step * nq, nq)
        kw = k_ref[0, pl.ds(key0, NAT_KEYS), :]
        vw = v_ref[0, pl.ds(key0, NAT_KEYS), :]
        q2 = q_ref[0, pl.ds(q0, nq), :]
        zero = jnp.zeros_like(q2)
        qm = jnp.concatenate([jnp.where(low_half, q2, zero), jnp.where(low_half, zero, q2)], axis=0)
        st = lax.dot_general(kw, qm, _NT, preferred_element_type=_F32)
        st = st + tab_ref[0, _natten_pattern(step, n_steps)]
        e = jnp.exp(st - jnp.max(st, axis=0, keepdims=True)).astype(_BF16)
        o2 = lax.dot_general(e, jnp.concatenate([vw, ones], axis=1), _TN,
                             preferred_element_type=_F32)
        o2 = o2[:, :LANES] / o2[:, LANES:]
        o_ref[0, pl.ds(q0, nq), :] = jnp.where(low_half, o2[:nq], o2[nq:]).astype(_BF16)

    def trip(i, carry):
        for u in range(NAT_UNROLL):
            one_step(i * NAT_UNROLL + u)
        return carry

    assert n_steps % NAT_UNROLL == 0
    lax.fori_loop(0, n_steps // NAT_UNROLL, trip, 0)


def _natten_bias_kernel(picked_ref, expand_ref, o_ref):
    o_ref[0] = jnp.dot(picked_ref[0], expand_ref[...], preferred_element_type=_F32,
                       precision=lax.Precision.HIGHEST)


def _natten_bias_table(rpb, rows):
    n_steps = rows // NAT_ROWS_PER_STEP
    assert rows % NAT_ROWS_PER_STEP == 0 and rows >= NAT_WIN_ROWS and n_steps >= NAT_PATTERNS
    steps = np.array([0, 1, 2, n_steps - 2, n_steps - 1])
    first = np.clip(NAT_ROWS_PER_STEP * steps - WIN_H // 2, 0, rows - WIN_H)
    win0 = np.minimum(first, rows - NAT_WIN_ROWS)
    i = np.arange(NAT_WIN_ROWS)[None, :, None, None, None]
    kc = np.arange(GRID_W)[None, None, :, None, None]
    j = np.arange(NAT_ROWS_PER_STEP)[None, None, None, :, None]
    qc = np.arange(GRID_W)[None, None, None, None, :]
    r = NAT_ROWS_PER_STEP * steps[:, None, None, None, None] + j
    key_row = win0[:, None, None, None, None] + i
    start = np.clip(r - WIN_H // 2, 0, rows - WIN_H)
    row_ok = (key_row >= start) & (key_row < start + WIN_H)
    col_start = np.clip(qc - WIN_W // 2, 0, GRID_W - WIN_W)
    col_ok = (kc >= col_start) & (kc < col_start + WIN_W)
    rel_row = np.clip(key_row - r + (WIN_H - 1), 0, 2 * WIN_H - 2)
    rel_col = np.clip(kc - qc + (WIN_W - 1), 0, 2 * WIN_W - 2)
    shape = (NAT_PATTERNS, NAT_WIN_ROWS, GRID_W, NAT_ROWS_PER_STEP, GRID_W)
    ok = np.broadcast_to(row_ok & col_ok, shape)
    n_heads, n_rel_col = rpb.shape[0], rpb.shape[2]
    row_sel = rel_row.reshape(-1)
    n_sel = -(-row_sel.size // SUBLANES) * SUBLANES
    picked = jnp.zeros((n_heads, n_sel, LANES), _F32)
    picked = picked.at[:, :row_sel.size, :n_rel_col].set(rpb.astype(_F32)[:, row_sel, :])
    expand = np.zeros((LANES, GRID_W * GRID_W), np.float32)
    expand[rel_col.reshape(-1), np.arange(GRID_W * GRID_W)] = 1.0
    tab = pl.pallas_call(
        _natten_bias_kernel,
        grid=(n_heads,),
        in_specs=[pl.BlockSpec((1, n_sel, LANES), lambda h: (h, 0, 0)),
                  _const_spec(expand.shape)],
        out_specs=pl.BlockSpec((1, n_sel, GRID_W * GRID_W), lambda h: (h, 0, 0)),
        out_shape=jax.ShapeDtypeStruct((n_heads, n_sel, GRID_W * GRID_W), _F32),
        compiler_params=_params("parallel"),
        name="natten_bias",
    )(picked, jnp.asarray(expand))
    nq = NAT_ROWS_PER_STEP * GRID_W
    tab = tab[:, :row_sel.size].reshape(N_PAIR, 2, NAT_PATTERNS, NAT_WIN_ROWS, NAT_ROWS_PER_STEP, GRID_W, GRID_W)
    tab = tab.transpose(0, 2, 3, 5, 1, 4, 6)
    tab = jnp.where(ok[None, :, :, :, None], tab, NEG_INF)
    return tab.reshape(N_PAIR, NAT_PATTERNS, NAT_KEYS, 2 * nq)


def _natten(proj3, table, run_after=None):
    b, s, _ = proj3.shape
    rows = s // GRID_W
    blk = (1, s, LANES)
    order_specs = [] if run_after is None else [pl.BlockSpec((SUBLANES, LANES), lambda p, i: (0, 0))]
    order_args = [] if run_after is None else [run_after]
    return pl.pallas_call(
        functools.partial(_natten_kernel, rows=rows),
        grid=(N_PAIR, b),
        in_specs=[pl.BlockSpec(blk, lambda p, i: (i, 0, p)),
                  pl.BlockSpec(blk, lambda p, i: (i, 0, N_PAIR + p)),
                  pl.BlockSpec(blk, lambda p, i: (i, 0, 2 * N_PAIR + p)),
                  pl.BlockSpec((1,) + table.shape[1:], lambda p, i: (p, 0, 0, 0))] + order_specs,
        out_specs=pl.BlockSpec(blk, lambda p, i: (i, 0, p)),
        out_shape=jax.ShapeDtypeStruct((b, s, TOK_WIDTH), _BF16),
        compiler_params=_params("parallel", "parallel"),
        name="natten",
    )(proj3, proj3, proj3, table, *order_args)


def _mixer_b_out_kernel(x_ref, tok_ref, qx_ref, k_ref, v_ref, wout_ref, g_ref, wr_ref,
                        o_ref, h_ref, idx_ref, gate_ref):
    x = _xattn_outproj(x_ref[0], tok_ref[0], qx_ref[0], k_ref[0], v_ref[0], wout_ref)
    o_ref[0] = x
    h = _rms(x, g_ref[...])
    h_ref[...] = _pack_halves(h)
    idx_ref[...], gate_ref[...] = _route(h, wr_ref)


def _mixer_b_out(x, tok, proj3, k, v, w_out, g_ffn, w_router2, *, tm):
    b, s, _ = x.shape
    t = b * s
    nj = s // tm
    qx_block = 3 * TOK_WIDTH // XA_WIDTH
    return pl.pallas_call(
        _mixer_b_out_kernel,
        grid=(b, nj),
        in_specs=[pl.BlockSpec((1, tm, D_MODEL), lambda i, j: (i, j, 0)),
                  pl.BlockSpec((1, tm, TOK_WIDTH), lambda i, j: (i, j, 0)),
                  pl.BlockSpec((1, tm, XA_WIDTH), lambda i, j: (i, j, qx_block)),
                  pl.BlockSpec((1, N_MEM, XA_WIDTH), lambda i, j: (i, 0, 0)),
                  pl.BlockSpec((1, N_MEM, XA_WIDTH), lambda i, j: (i, 0, 0)),
                  _const_spec((D_MODEL, D_MODEL)),
                  _const_spec((1, D_MODEL)),
                  _const_spec((D_MODEL, 2 * LANES))],
        out_specs=[pl.BlockSpec((1, tm, D_MODEL), lambda i, j: (i, j, 0)),
                   pl.BlockSpec((tm, D_MODEL // 2), lambda i, j: (i * nj + j, 0)),
                   pl.BlockSpec((SUBLANES, tm), lambda i, j: (0, i * nj + j)),
                   pl.BlockSpec((SUBLANES, tm), lambda i, j: (0, i * nj + j))],
        out_shape=[jax.ShapeDtypeStruct(x.shape, _F32),
                   jax.ShapeDtypeStruct((t, D_MODEL // 2), jnp.uint32),
                   jax.ShapeDtypeStruct((SUBLANES, t), jnp.int32),
                   jax.ShapeDtypeStruct((SUBLANES, t), _F32)],
        compiler_params=_params("parallel", "parallel"),
        name="mixer_natten_out",
    )(x, tok, proj3, k, v, w_out, g_ffn, w_router2)


def _pack_halves(h):
    half = D_MODEL // 2
    hi = pltpu.bitcast(h[:, :half].astype(_BF16).astype(_F32), jnp.uint32)
    lo = pltpu.bitcast(h[:, half:].astype(_BF16).astype(_F32), jnp.uint32)
    return hi | (lo >> 16)


def _unpack_halves(w):
    hi = pltpu.bitcast(w & jnp.uint32(0xFFFF0000), _F32).astype(_BF16)
    lo = pltpu.bitcast(w << 16, _F32).astype(_BF16)
    return jnp.concatenate([hi, lo], axis=1)


def _route(h, wr_ref):
    h_head = h.astype(_BF16)
    h_rest = (h - h_head.astype(_F32)).astype(_BF16)
    both = _dot(h_head, wr_ref[...])
    logits = both[:, :LANES] + both[:, LANES:] + _dot(h_rest, wr_ref[:, :LANES])
    assert N_EXPERTS == SUBLANES
    lt = jnp.concatenate([jnp.transpose(logits[c * LANES:(c + 1) * LANES])[:SUBLANES]
                          for c in range(h.shape[0] // LANES)], axis=1)
    expert = lax.broadcasted_iota(jnp.int32, lt.shape, 0)
    v1 = jnp.max(lt, axis=0, keepdims=True)
    i1 = jnp.min(jnp.where(lt == v1, expert, N_EXPERTS), axis=0, keepdims=True)
    lt2 = jnp.where(expert == i1, -jnp.inf, lt)
    v2 = jnp.max(lt2, axis=0, keepdims=True)
    i2 = jnp.min(jnp.where(lt2 == v2, expert, N_EXPERTS), axis=0, keepdims=True)
    e2 = jnp.exp(v2 - v1)
    g1 = 1.0 / (1.0 + e2)
    g2 = e2 / (1.0 + e2)
    idx = jnp.where(expert == 0, i1, jnp.where(expert == 1, i2, 0))
    gate = jnp.where(expert == 0, g1, jnp.where(expert == 1, g2, 0.0))
    return idx, gate


def _gather_rows(table, idx):
    n, d = idx.shape[0], table.shape[1]
    workers = SC_CORES * SC_SUBCORES
    per_worker = n // workers
    pairs = per_worker // (2 * SC_WINDOW)
    assert n == workers * pairs * 2 * SC_WINDOW, (n, workers, SC_WINDOW)
    mesh = plsc.VectorSubcoreMesh(core_axis_name="core", subcore_axis_name="subcore",
                                  num_cores=SC_CORES, num_subcores=SC_SUBCORES)

    def gather_kernel(table_hbm, idx_hbm, out_hbm, idx_a, idx_b, rows_a, rows_b, sem_a, sem_b):
        worker = lax.axis_index("subcore") * SC_CORES + lax.axis_index("core")
        base = worker * per_worker

        @pl.loop(0, pairs)
        def _(p):
            r0 = base + p * (2 * SC_WINDOW)
            r1 = r0 + SC_WINDOW
            pltpu.sync_copy(idx_hbm.at[pl.ds(r0, SC_WINDOW)], idx_a)
            gather_a = pltpu.async_copy(table_hbm.at[idx_a], rows_a, sem_a)
            pltpu.sync_copy(idx_hbm.at[pl.ds(r1, SC_WINDOW)], idx_b)
            gather_b = pltpu.async_copy(table_hbm.at[idx_b], rows_b, sem_b)
            gather_a.wait()
            pltpu.sync_copy(rows_a, out_hbm.at[pl.ds(r0, SC_WINDOW)])
            gather_b.wait()
            pltpu.sync_copy(rows_b, out_hbm.at[pl.ds(r1, SC_WINDOW)])

    idx_buf = pltpu.VMEM((SC_WINDOW,), jnp.int32)
    row_buf = pltpu.VMEM((SC_WINDOW, d), table.dtype)
    return pl.kernel(gather_kernel, out_type=jax.ShapeDtypeStruct((n, d), table.dtype), mesh=mesh,
                     scratch_types=[idx_buf, idx_buf, row_buf, row_buf,
                                    pltpu.SemaphoreType.DMA, pltpu.SemaphoreType.DMA],
                     name="moe_row_gather")(table, idx)


def _expert_kernel(meta_ref, xs_ref, wgu_ref, wd_ref, ys_ref, *, n_blocks):
    @pl.when(pl.program_id(0) < meta_ref[n_blocks])
    def _():
        h = _unpack_halves(xs_ref[...])
        acc = jnp.zeros((MOE_ROWS, D_MODEL), _F32)
        for c in range(D_FF_EXPERT // FF_CHUNK_EXPERT):
            lo = c * FF_CHUNK_EXPERT
            gate = _dot(h, wgu_ref[0, :, lo:lo + FF_CHUNK_EXPERT])
            up = _dot(h, wgu_ref[0, :, D_FF_EXPERT + lo:D_FF_EXPERT + lo + FF_CHUNK_EXPERT])
            a = (_silu(gate) * up).astype(_BF16)
            acc = acc + _dot(a, wd_ref[0, lo:lo + FF_CHUNK_EXPERT, :])
        ys_ref[...] = _pack_halves(acc)


def _experts(meta, xs, w_gu, w_down):
    n_rows, half = xs.shape
    n_blocks = n_rows // MOE_ROWS
    grid_spec = pltpu.PrefetchScalarGridSpec(
        num_scalar_prefetch=1,
        grid=(n_blocks,),
        in_specs=[pl.BlockSpec((MOE_ROWS, half), lambda i, m: (i, 0)),
                  pl.BlockSpec((1, D_MODEL, 2 * D_FF_EXPERT), lambda i, m: (m[i], 0, 0),
                               pipeline_mode=pl.Buffered(1)),
                  pl.BlockSpec((1, D_FF_EXPERT, D_MODEL), lambda i, m: (m[i], 0, 0),
                               pipeline_mode=pl.Buffered(1))],
        out_specs=pl.BlockSpec((MOE_ROWS, half), lambda i, m: (i, 0)),
    )
    return pl.pallas_call(
        functools.partial(_expert_kernel, n_blocks=n_blocks),
        grid_spec=grid_spec,
        out_shape=jax.ShapeDtypeStruct((n_rows, half), jnp.uint32),
        compiler_params=_params("arbitrary"),
        name="moe_experts",
    )(meta, xs, w_gu, w_down)


def _combine_kernel(x_ref, y0_ref, y1_ref, gate_ref, g_ref, o_ref, *, tc):
    gates = gate_ref[...]
    pad = jnp.zeros((LANES - SUBLANES, LANES), _F32)
    cols = [jnp.transpose(jnp.concatenate([gates[:, c * LANES:(c + 1) * LANES], pad], axis=0))
            for c in range(tc // LANES)]
    gcol = jnp.concatenate(cols, axis=0)
    y0 = _unpack_halves(y0_ref[...]).astype(_F32)
    y1 = _unpack_halves(y1_ref[...]).astype(_F32)
    f = y0 * gcol[:, 0:1] + y1 * gcol[:, 1:2]
    o_ref[...] = _rms(x_ref[...] + f, g_ref[...])


def _combine(x2, y0, y1, gates, g_final, *, tc):
    t = x2.shape[0]
    half = y0.shape[1]
    return pl.pallas_call(
        functools.partial(_combine_kernel, tc=tc),
        grid=(t // tc,),
        in_specs=[pl.BlockSpec((tc, D_MODEL), lambda i: (i, 0)),
                  pl.BlockSpec((tc, half), lambda i: (i, 0)),
                  pl.BlockSpec((tc, half), lambda i: (i, 0)),
                  pl.BlockSpec((SUBLANES, tc), lambda i: (0, i)),
                  _const_spec((1, D_MODEL))],
        out_specs=pl.BlockSpec((tc, D_MODEL), lambda i: (i, 0)),
        out_shape=jax.ShapeDtypeStruct(x2.shape, _F32),
        compiler_params=_params("parallel"),
        name="moe_combine",
    )(x2, y0, y1, gates, g_final)


def _routing_offsets(expert_idx, n_blocks):
    t = expert_idx.shape[1]
    n = 2 * t
    flat_e = expert_idx[:2].reshape(n)
    experts = jnp.arange(N_EXPERTS, dtype=jnp.int32)
    onehot = (flat_e[:, None] == experts[None, :]).astype(jnp.int32)
    running = jnp.cumsum(onehot, axis=0)
    rank = jnp.sum((running - onehot) * onehot, axis=1)
    counts = running[-1]
    padded = (counts + MOE_ROWS - 1) // MOE_ROWS * MOE_ROWS
    pad_end = jnp.cumsum(padded)
    pad_start = pad_end - padded
    start = jnp.cumsum(counts) - counts
    dest = (jnp.sum(onehot * pad_start[None, :], axis=1) + rank).astype(jnp.int32)

    block_row = jnp.arange(n_blocks, dtype=jnp.int32) * MOE_ROWS
    block_expert = jnp.minimum(jnp.sum((block_row[:, None] >= pad_end[None, :]).astype(jnp.int32), axis=1),
                               N_EXPERTS - 1)
    meta = jnp.concatenate([block_expert, pad_end[-1:] // MOE_ROWS]).astype(jnp.int32)

    assert N_EXPERTS * n < 2 ** 31
    order = jnp.sort(flat_e * n + jnp.arange(n, dtype=jnp.int32)) % n
    row_onehot = jnp.repeat((block_expert[:, None] == experts[None, :]).astype(jnp.int32), MOE_ROWS, axis=0)
    off = jnp.arange(n_blocks * MOE_ROWS, dtype=jnp.int32) - jnp.sum(row_onehot * pad_start[None, :], axis=1)
    valid = off < jnp.sum(row_onehot * counts[None, :], axis=1)
    pos = jnp.clip(jnp.sum(row_onehot * start[None, :], axis=1) + off, 0, n - 1)
    src_tok = jnp.where(valid, order[pos] % t, 0).astype(jnp.int32)
    return dest, src_tok, meta


def _dense_layers(x, mem, w, w_experts=None, run_after=None):
    b, s, d = x.shape
    t = b * s

    k0, v0, k1, v1 = _mem_kv(mem, w['norm_mem'], w['w_mem_kv'])
    x = _mixer_a(x, k0, v0, w['norm_mix'][0], w['w_in_a'], w['g_v_a'], w['w_s_a'], w['b_s_a'],
                 w['w_out'][0], tm=MIXER_TILE)
    side = () if w_experts is not None else (w['w_exp_gu'], w['w_exp_down'])
    x2, cast = _ffn(x.reshape(t, d), w['norm_ffn'][0], w['w_ffn_gu'], w['w_ffn_down'], side, tm=ROW_TILE)
    w_exp_gu, w_exp_down = w_experts if w_experts is not None else cast

    proj = _proj_b(x2, w['norm_mix'][1], w['w_in_b'], tm=ROW_TILE).reshape(b, s, -1)
    tok = _natten(proj, _natten_bias_table(w['rpb_b'], s // GRID_W), run_after)
    x, h_packed, expert_idx, gates = _mixer_b_out(x2.reshape(b, s, d), tok, proj, k1, v1, w['w_out'][1],
                                                  w['norm_ffn'][1], w['w_router'], tm=MIXER_TILE)
    return x.reshape(t, d), h_packed, expert_idx, gates, (w_exp_gu, w_exp_down)


def _dispatch(h_packed, expert_idx):
    t = h_packed.shape[0]
    n_blocks = 2 * t // MOE_ROWS + N_EXPERTS
    dest, src_tok, meta = _routing_offsets(expert_idx, n_blocks)
    return _gather_rows(h_packed, src_tok), dest, src_tok, meta


def _experts_combine(x2, gates, xs, dest, meta, w_experts, g_final):
    t = x2.shape[0]
    ys = _experts(meta, xs, *w_experts)
    y0 = _gather_rows(ys, dest[:t])
    y1 = _gather_rows(ys, dest[t:])
    return _combine(x2, y0, y1, gates, g_final, tc=ROW_TILE)


def _cast_kernel(w_ref, o_ref):
    o_ref[...] = w_ref[...].astype(o_ref.dtype)


def _to_bf16(w, rows_per_step):
    n, r, c = w.shape
    assert r % rows_per_step == 0
    block = (1, rows_per_step, c)
    return pl.pallas_call(
        _cast_kernel,
        grid=(n, r // rows_per_step),
        in_specs=[pl.BlockSpec(block, lambda i, j: (i, j, 0))],
        out_specs=pl.BlockSpec(block, lambda i, j: (i, j, 0)),
        out_shape=jax.ShapeDtypeStruct(w.shape, _BF16),
        compiler_params=_params("parallel", "parallel"),
        name="weight_cast",
    )(w)


def _split_router_weight(w):
    head = w.astype(_BF16)
    rest = (w - head.astype(_F32)).astype(_BF16)
    pad = ((0, 0), (0, LANES - N_EXPERTS))
    return jnp.concatenate([jnp.pad(head, pad), jnp.pad(rest, pad)], axis=1)


def _prepare_weights(norm_mix, norm_mem, norm_ffn, norm_final, w_in_a, g_v_a, w_s_a, b_s_a, w_in_b, rpb_b,
                     w_mem_kv, w_out, w_ffn_gu, w_ffn_down, w_router, w_exp_gu, w_exp_down):
    assert w_in_a.shape[0] == 1 and w_in_b.shape[0] == 1 and norm_mix.shape[0] == 2
    row = lambda a: a.astype(_F32)[:, None, :]
    return {
        'norm_mix': row(norm_mix), 'norm_mem': row(norm_mem), 'norm_ffn': row(norm_ffn),
        'norm_final': norm_final.astype(_F32)[None, :],
        'w_in_a': w_in_a[0].astype(_BF16),
        'g_v_a': g_v_a.astype(_F32),
        'w_s_a': jnp.concatenate([w_s_a[0, 0::2], w_s_a[0, 1::2]], axis=2).astype(_BF16),
        'b_s_a': jnp.repeat(b_s_a[0].astype(_F32).T, HEAD_DIM, axis=1),
        'w_in_b': w_in_b[0].astype(_BF16),
        'rpb_b': rpb_b[0],
        'w_mem_kv': w_mem_kv.astype(_BF16),
        'w_out': w_out.astype(_BF16),
        'w_ffn_gu': _to_bf16(w_ffn_gu, 256)[0],
        'w_ffn_down': _to_bf16(w_ffn_down, 704)[0],
        'w_router': _split_router_weight(w_router[0].astype(_F32)),
        'w_exp_gu': w_exp_gu[0].astype(_F32),
        'w_exp_down': w_exp_down[0].astype(_F32),
    }


def kernel(x_prompt, x_sample, mem_prompt, mem_sample, norm_mix, norm_mem, norm_ffn, norm_final, w_in_a, g_v_a, w_s_a, b_s_a, w_in_b, rpb_b, w_mem_kv, w_out, w_ffn_gu, w_ffn_down, w_router, w_exp_gu, w_exp_down):
    w = _prepare_weights(norm_mix, norm_mem, norm_ffn, norm_final, w_in_a, g_v_a, w_s_a, b_s_a, w_in_b, rpb_b,
                         w_mem_kv, w_out, w_ffn_gu, w_ffn_down, w_router, w_exp_gu, w_exp_down)
    x2_p, h_p, idx_p, gates_p, w_experts = _dense_layers(x_prompt, mem_prompt, w)
    xs_p, dest_p, _, meta_p = _dispatch(h_p, idx_p)
    x2_s, h_s, idx_s, gates_s, _ = _dense_layers(x_sample, mem_sample, w, w_experts, run_after=xs_p)
    xs_s, dest_s, _, meta_s = _dispatch(h_s, idx_s)
    y_prompt = _experts_combine(x2_p, gates_p, xs_p, dest_p, meta_p, w_experts, w['norm_final'])
    y_sample = _experts_combine(x2_s, gates_s, xs_s, dest_s, meta_s, w_experts, w['norm_final'])
    return (y_prompt.reshape(x_prompt.shape), y_sample.reshape(x_sample.shape))
```

```python
import functools

import numpy as np
import jax
import jax.numpy as jnp
from jax import lax
from jax.experimental import pallas as pl
from jax.experimental.pallas import tpu as pltpu
from jax.experimental.pallas import tpu_sc as plsc

D_MODEL = 1024
HEAD_DIM = 64
XA_HEADS = 4
XA_WIDTH = XA_HEADS * HEAD_DIM
TOK_WIDTH = D_MODEL - XA_WIDTH
CHUNK = 128
N_PAIR = TOK_WIDTH // (2 * HEAD_DIM)
GRID_W = 64
WIN_H = 8
WIN_W = 16
N_MEM = 256
D_FF = 2816
N_EXPERTS = 8
D_FF_EXPERT = 3584
RMS_EPS = 1e-6
NEG_INF = -1e30
ATTN_SCALE = HEAD_DIM ** -0.5

LANES = 128
SUBLANES = 8
VMEM_LIMIT = 56 * 1024 * 1024

NAT_ROWS_PER_STEP = 2
NAT_WIN_ROWS = WIN_H + NAT_ROWS_PER_STEP - 1
NAT_KEYS = NAT_WIN_ROWS * GRID_W
NAT_PATTERNS = 5
NAT_UNROLL = 4

MOE_ROWS = 512
SC_CORES = 2
SC_SUBCORES = 16
SC_WINDOW = 64
MXU_WIDTH = 256
FF_SPLIT_DENSE = 6 * MXU_WIDTH
ROW_TILE = 512
MIXER_TILE = 1024
FF_CHUNK_EXPERT = 512

_BF16 = jnp.bfloat16
_F32 = jnp.float32
_NT = (((1,), (1,)), ((), ()))
_TN = (((0,), (0,)), ((), ()))


def _params(*semantics):
    return pltpu.CompilerParams(dimension_semantics=semantics, vmem_limit_bytes=VMEM_LIMIT)


def _const_spec(shape):
    zeros = (0,) * len(shape)
    return pl.BlockSpec(shape, lambda *_: zeros, pipeline_mode=pl.Buffered(1))


def _rms(x, g):
    ms = jnp.mean(x * x, axis=-1, keepdims=True)
    return x * lax.rsqrt(ms + RMS_EPS) * g


def _dot(a, b):
    return jnp.dot(a, b, preferred_element_type=_F32)


def _gelu(x):
    return 0.5 * x * (1.0 + lax.erf(x * np.float32(1.0 / np.sqrt(2.0))))


def _silu(x):
    return x / (1.0 + jnp.exp(-x))


def _softmax_rows(s):
    m = jnp.max(s, axis=-1, keepdims=True)
    e = jnp.exp(s - m)
    return e / jnp.sum(e, axis=-1, keepdims=True)


def _xattn_outproj(x, tok, qx, k, v, wout_ref):
    lane = lax.broadcasted_iota(jnp.int32, (1, XA_WIDTH), 1)
    qx = qx.astype(_F32)
    xa = jnp.zeros(qx.shape, _F32)
    for h in range(XA_HEADS):
        in_head = (lane >= h * HEAD_DIM) & (lane < (h + 1) * HEAD_DIM)
        qh = jnp.where(in_head, qx, 0.0).astype(_BF16)
        s = lax.dot_general(qh, k, _NT, preferred_element_type=_F32)
        p = _softmax_rows(s).astype(_BF16)
        xa = jnp.where(in_head, _dot(p, v), xa)
    y = _dot(tok, wout_ref[:TOK_WIDTH, :]) + _dot(xa.astype(_BF16), wout_ref[TOK_WIDTH:, :])
    return x + y


def _mem_kv_kernel(mem_ref, g_ref, w_ref, *kv_refs):
    mem = mem_ref[0]
    for layer in range(w_ref.shape[0]):
        m = _rms(mem, g_ref[layer]).astype(_BF16)
        kv = _dot(m, w_ref[layer])
        kv_refs[2 * layer][0] = kv[:, :XA_WIDTH].astype(_BF16)
        kv_refs[2 * layer + 1][0] = kv[:, XA_WIDTH:].astype(_BF16)


def _mem_kv(mem, g, w_kv):
    b = mem.shape[0]
    layers = w_kv.shape[0]
    out = jax.ShapeDtypeStruct((b, N_MEM, XA_WIDTH), _BF16)
    return pl.pallas_call(
        _mem_kv_kernel,
        grid=(b,),
        in_specs=[pl.BlockSpec((1, N_MEM, D_MODEL), lambda i: (i, 0, 0)),
                  _const_spec((layers, 1, D_MODEL)),
                  _const_spec((layers, D_MODEL, 2 * XA_WIDTH))],
        out_specs=[pl.BlockSpec((1, N_MEM, XA_WIDTH), lambda i: (i, 0, 0))] * (2 * layers),
        out_shape=[out] * (2 * layers),
        compiler_params=_params("parallel"),
        name="mem_kv",
    )(mem, g, w_kv)


def _mixer_a_kernel(x_ref, k_ref, v_ref, g_ref, win_ref, gv_ref, ws_ref, bs_ref, wout_ref, o_ref, *, tm):
    assert tm % (2 * CHUNK) == 0
    x = x_ref[0]
    h = _rms(x, g_ref[...]).astype(_BF16)
    proj = _dot(h, win_ref[...])
    z = _gelu(proj[:, :2 * TOK_WIDTH])
    u = z[:, :TOK_WIDTH]
    vn = _rms(z[:, TOK_WIDTH:], gv_ref[...]).astype(_BF16)
    low_half = lax.broadcasted_iota(jnp.int32, (1, LANES), 1) < HEAD_DIM
    bias = bs_ref[...]
    zero = jnp.zeros((CHUNK, LANES), _BF16)
    toks = []
    for c in range(0, tm // CHUNK, 2):
        va, vb = vn[c * CHUNK:(c + 1) * CHUNK], vn[(c + 1) * CHUNK:(c + 2) * CHUNK]
        cols_a, cols_b = [], []
        for j in range(N_PAIR):
            pa, pb = va[:, j * LANES:(j + 1) * LANES], vb[:, j * LANES:(j + 1) * LANES]
            even = jnp.concatenate([jnp.where(low_half, pa, zero), jnp.where(low_half, pb, zero)], axis=1)
            odd = jnp.concatenate([jnp.where(low_half, zero, pa), jnp.where(low_half, zero, pb)], axis=1)
            r = _dot(ws_ref[j], jnp.concatenate([even, odd], axis=0))
            cols_a.append(r[:, :LANES])
            cols_b.append(r[:, LANES:])
        for side, cols in enumerate((cols_a, cols_b)):
            sv = jnp.concatenate(cols, axis=1) + bias
            toks.append(u[(c + side) * CHUNK:(c + side + 1) * CHUNK] * sv)
    tok = jnp.concatenate(toks, axis=0).astype(_BF16)
    qx = proj[:, 2 * TOK_WIDTH:] * ATTN_SCALE
    o_ref[0] = _xattn_outproj(x, tok, qx, k_ref[0], v_ref[0], wout_ref)


def _mixer_a(x, k, v, g, w_in, g_v, w_s2, b_s2, w_out, *, tm):
    b, s, _ = x.shape
    return pl.pallas_call(
        functools.partial(_mixer_a_kernel, tm=tm),
        grid=(b, s // tm),
        in_specs=[pl.BlockSpec((1, tm, D_MODEL), lambda i, j: (i, j, 0)),
                  pl.BlockSpec((1, N_MEM, XA_WIDTH), lambda i, j: (i, 0, 0)),
                  pl.BlockSpec((1, N_MEM, XA_WIDTH), lambda i, j: (i, 0, 0)),
                  _const_spec((1, D_MODEL)),
                  _const_spec((D_MODEL, 2 * TOK_WIDTH + XA_WIDTH)),
                  _const_spec((1, TOK_WIDTH)),
                  _const_spec((N_PAIR, CHUNK, 2 * CHUNK)),
                  _const_spec((CHUNK, TOK_WIDTH)),
                  _const_spec((D_MODEL, D_MODEL))],
        out_specs=pl.BlockSpec((1, tm, D_MODEL), lambda i, j: (i, j, 0)),
        out_shape=jax.ShapeDtypeStruct(x.shape, _F32),
        compiler_params=_params("parallel", "parallel"),
        name="mixer_gmlp",
    )(x, k, v, g, w_in, g_v, w_s2, b_s2, w_out)


def _ffn_kernel(x_ref, g_ref, wgu_ref, wd_ref, *refs):
    n_side = (len(refs) - 1) // 2
    o_ref = refs[n_side]
    x = x_ref[...]
    h = _rms(x, g_ref[...]).astype(_BF16)
    acc = x
    for lo, hi in ((0, FF_SPLIT_DENSE), (FF_SPLIT_DENSE, D_FF)):
        gate = _dot(h, wgu_ref[:, lo:hi])
        up = _dot(h, wgu_ref[:, D_FF + lo:D_FF + hi])
        a = (_silu(gate) * up).astype(_BF16)
        acc = acc + _dot(a, wd_ref[lo:hi, :])
    o_ref[...] = acc
    for src, dst in zip(refs[:n_side], refs[n_side + 1:]):
        dst[...] = src[...].astype(dst.dtype)


def _ffn(x2, g, w_gu, w_down, side_casts=(), *, tm):
    t = x2.shape[0]
    steps = t // tm
    flat = [w.reshape(-1, w.shape[-1]) for w in side_casts]
    slabs = [(w.shape[0] // steps, w.shape[1]) for w in flat]
    assert all(w.shape[0] == s[0] * steps and s[0] % (2 * SUBLANES) == 0 for w, s in zip(flat, slabs))
    side_specs = [pl.BlockSpec(s, lambda i: (i, 0)) for s in slabs]
    outs = pl.pallas_call(
        _ffn_kernel,
        grid=(steps,),
        in_specs=[pl.BlockSpec((tm, D_MODEL), lambda i: (i, 0)),
                  _const_spec((1, D_MODEL)),
                  _const_spec((D_MODEL, 2 * D_FF)),
                  _const_spec((D_FF, D_MODEL))] + side_specs,
        out_specs=[pl.BlockSpec((tm, D_MODEL), lambda i: (i, 0))] + side_specs,
        out_shape=[jax.ShapeDtypeStruct(x2.shape, _F32)] + [jax.ShapeDtypeStruct(w.shape, _BF16) for w in flat],
        compiler_params=_params("parallel"),
        name="ffn_dense",
    )(x2, g, w_gu, w_down, *flat)
    return outs[0], [o.reshape(w.shape) for o, w in zip(outs[1:], side_casts)]


def _proj_b_kernel(x_ref, g_ref, w_ref, o_ref):
    h = _rms(x_ref[...], g_ref[...]).astype(_BF16)
    proj = _dot(h, w_ref[...])
    col = lax.broadcasted_iota(jnp.int32, (1, proj.shape[1]), 1)
    is_query = (col < TOK_WIDTH) | (col >= 3 * TOK_WIDTH)
    o_ref[...] = (proj * jnp.where(is_query, ATTN_SCALE, 1.0)).astype(_BF16)


def _proj_b(x2, g, w_in, *, tm):
    t = x2.shape[0]
    n = w_in.shape[1]
    return pl.pallas_call(
        _proj_b_kernel,
        grid=(t // tm,),
        in_specs=[pl.BlockSpec((tm, D_MODEL), lambda i: (i, 0)),
                  _const_spec((1, D_MODEL)),
                  _const_spec((D_MODEL, n))],
        out_specs=pl.BlockSpec((tm, n), lambda i: (i, 0)),
        out_shape=jax.ShapeDtypeStruct((t, n), _BF16),
        compiler_params=_params("parallel"),
        name="proj_natten",
    )(x2, g, w_in)


def _natten_pattern(step, n_steps):
    return jnp.where(step < 2, step, jnp.where(step >= n_steps - 2, step - n_steps + NAT_PATTERNS, 2))


def _natten_window_start(step, rows):
    first = jnp.clip(NAT_ROWS_PER_STEP * step - WIN_H // 2, 0, rows - WIN_H)
    return jnp.minimum(first, rows - NAT_WIN_ROWS)


def _natten_kernel(q_ref, k_ref, v_ref, tab_ref, *rest, rows):
    o_ref = rest[-1]
    n_steps = rows // NAT_ROWS_PER_STEP
    nq = NAT_ROWS_PER_STEP * GRID_W
    low_half = lax.broadcasted_iota(jnp.int32, (1, LANES), 1) < HEAD_DIM

    ones = jnp.ones((NAT_KEYS, LANES), _BF16)

    def one_step(step):
        key0 = pl.multiple_of(_natten_window_start(step, rows) * GRID_W, GRID_W)
        q0 = pl.multiple_of(step * nq, nq)
        kw = k_ref[0, pl.ds(key0, NAT_KEYS), :]
        vw = v_ref[0, pl.ds(key0, NAT_KEYS), :]
        q2 = q_ref[0, pl.ds(q0, nq), :]
        zero = jnp.zeros_like(q2)
        qm = jnp.concatenate([jnp.where(low_half, q2, zero), jnp.where(low_half, zero, q2)], axis=0)
        st = lax.dot_general(kw, qm, _NT, preferred_element_type=_F32)
        st = st + tab_ref[0, _natten_pattern(step, n_steps)]
        e = jnp.exp(st - jnp.max(st, axis=0, keepdims=True)).astype(_BF16)
        o2 = lax.dot_general(e, jnp.concatenate([vw, ones], axis=1), _TN,
                             preferred_element_type=_F32)
        o2 = o2[:, :LANES] / o2[:, LANES:]
        o_ref[0, pl.ds(q0, nq), :] = jnp.where(low_half, o2[:nq], o2[nq:]).astype(_BF16)

    def trip(i, carry):
        for u in range(NAT_UNROLL):
            one_step(i * NAT_UNROLL + u)
        return carry

    assert n_steps % NAT_UNROLL == 0
    lax.fori_loop(0, n_steps // NAT_UNROLL, trip, 0)


def _natten_bias_kernel(picked_ref, expand_ref, o_ref):
    o_ref[0] = jnp.dot(picked_ref[0], expand_ref[...], preferred_element_type=_F32,
                       precision=lax.Precision.HIGHEST)


def _natten_bias_table(rpb, rows):
    n_steps = rows // NAT_ROWS_PER_STEP
    assert rows % NAT_ROWS_PER_STEP == 0 and rows >= NAT_WIN_ROWS and n_steps >= NAT_PATTERNS
    steps = np.array([0, 1, 2, n_steps - 2, n_steps - 1])
    first = np.clip(NAT_ROWS_PER_STEP * steps - WIN_H // 2, 0, rows - WIN_H)
    win0 = np.minimum(first, rows - NAT_WIN_ROWS)
    i = np.arange(NAT_WIN_ROWS)[None, :, None, None, None]
    kc = np.arange(GRID_W)[None, None, :, None, None]
    j = np.arange(NAT_ROWS_PER_STEP)[None, None, None, :, None]
    qc = np.arange(GRID_W)[None, None, None, None, :]
    r = NAT_ROWS_PER_STEP * steps[:, None, None, None, None] + j
    key_row = win0[:, None, None, None, None] + i
    start = np.clip(r - WIN_H // 2, 0, rows - WIN_H)
    row_ok = (key_row >= start) & (key_row < start + WIN_H)
    col_start = np.clip(qc - WIN_W // 2, 0, GRID_W - WIN_W)
    col_ok = (kc >= col_start) & (kc < col_start + WIN_W)
    rel_row = np.clip(key_row - r + (WIN_H - 1), 0, 2 * WIN_H - 2)
    rel_col = np.clip(kc - qc + (WIN_W - 1), 0, 2 * WIN_W - 2)
    shape = (NAT_PATTERNS, NAT_WIN_ROWS, GRID_W, NAT_ROWS_PER_STEP, GRID_W)
    ok = np.broadcast_to(row_ok & col_ok, shape)
    n_heads, n_rel_col = rpb.shape[0], rpb.shape[2]
    row_sel = rel_row.reshape(-1)
    n_sel = -(-row_sel.size // SUBLANES) * SUBLANES
    picked = jnp.zeros((n_heads, n_sel, LANES), _F32)
    picked = picked.at[:, :row_sel.size, :n_rel_col].set(rpb.astype(_F32)[:, row_sel, :])
    expand = np.zeros((LANES, GRID_W * GRID_W), np.float32)
    expand[rel_col.reshape(-1), np.arange(GRID_W * GRID_W)] = 1.0
    tab = pl.pallas_call(
        _natten_bias_kernel,
        grid=(n_heads,),
        in_specs=[pl.BlockSpec((1, n_sel, LANES), lambda h: (h, 0, 0)),
                  _const_spec(expand.shape)],
        out_specs=pl.BlockSpec((1, n_sel, GRID_W * GRID_W), lambda h: (h, 0, 0)),
        out_shape=jax.ShapeDtypeStruct((n_heads, n_sel, GRID_W * GRID_W), _F32),
        compiler_params=_params("parallel"),
        name="natten_bias",
    )(picked, jnp.asarray(expand))
    nq = NAT_ROWS_PER_STEP * GRID_W
    tab = tab[:, :row_sel.size].reshape(N_PAIR, 2, NAT_PATTERNS, NAT_WIN_ROWS, NAT_ROWS_PER_STEP, GRID_W, GRID_W)
    tab = tab.transpose(0, 2, 3, 5, 1, 4, 6)
    tab = jnp.where(ok[None, :, :, :, None], tab, NEG_INF)
    return tab.reshape(N_PAIR, NAT_PATTERNS, NAT_KEYS, 2 * nq)


def _natten(proj3, table, run_after=None):
    b, s, _ = proj3.shape
    rows = s // GRID_W
    blk = (1, s, LANES)
    order_specs = [] if run_after is None else [pl.BlockSpec((SUBLANES, LANES), lambda p, i: (0, 0))]
    order_args = [] if run_after is None else [run_after]
    return pl.pallas_call(
        functools.partial(_natten_kernel, rows=rows),
        grid=(N_PAIR, b),
        in_specs=[pl.BlockSpec(blk, lambda p, i: (i, 0, p)),
                  pl.BlockSpec(blk, lambda p, i: (i, 0, N_PAIR + p)),
                  pl.BlockSpec(blk, lambda p, i: (i, 0, 2 * N_PAIR + p)),
                  pl.BlockSpec((1,) + table.shape[1:], lambda p, i: (p, 0, 0, 0))] + order_specs,
        out_specs=pl.BlockSpec(blk, lambda p, i: (i, 0, p)),
        out_shape=jax.ShapeDtypeStruct((b, s, TOK_WIDTH), _BF16),
        compiler_params=_params("parallel", "parallel"),
        name="natten",
    )(proj3, proj3, proj3, table, *order_args)


def _mixer_b_out_kernel(x_ref, tok_ref, qx_ref, k_ref, v_ref, wout_ref, g_ref, wr_ref,
                        o_ref, h_ref, idx_ref, gate_ref):
    x = _xattn_outproj(x_ref[0], tok_ref[0], qx_ref[0], k_ref[0], v_ref[0], wout_ref)
    o_ref[0] = x
    h = _rms(x, g_ref[...])
    h_ref[...] = _pack_halves(h)
    idx_ref[...], gate_ref[...] = _route(h, wr_ref)


def _mixer_b_out(x, tok, proj3, k, v, w_out, g_ffn, w_router2, *, tm):
    b, s, _ = x.shape
    t = b * s
    nj = s // tm
    qx_block = 3 * TOK_WIDTH // XA_WIDTH
    return pl.pallas_call(
        _mixer_b_out_kernel,
        grid=(b, nj),
        in_specs=[pl.BlockSpec((1, tm, D_MODEL), lambda i, j: (i, j, 0)),
                  pl.BlockSpec((1, tm, TOK_WIDTH), lambda i, j: (i, j, 0)),
                  pl.BlockSpec((1, tm, XA_WIDTH), lambda i, j: (i, j, qx_block)),
                  pl.BlockSpec((1, N_MEM, XA_WIDTH), lambda i, j: (i, 0, 0)),
                  pl.BlockSpec((1, N_MEM, XA_WIDTH), lambda i, j: (i, 0, 0)),
                  _const_spec((D_MODEL, D_MODEL)),
                  _const_spec((1, D_MODEL)),
                  _const_spec((D_MODEL, 2 * LANES))],
        out_specs=[pl.BlockSpec((1, tm, D_MODEL), lambda i, j: (i, j, 0)),
                   pl.BlockSpec((tm, D_MODEL // 2), lambda i, j: (i * nj + j, 0)),
                   pl.BlockSpec((SUBLANES, tm), lambda i, j: (0, i * nj + j)),
                   pl.BlockSpec((SUBLANES, tm), lambda i, j: (0, i * nj + j))],
        out_shape=[jax.ShapeDtypeStruct(x.shape, _F32),
                   jax.ShapeDtypeStruct((t, D_MODEL // 2), jnp.uint32),
                   jax.ShapeDtypeStruct((SUBLANES, t), jnp.int32),
                   jax.ShapeDtypeStruct((SUBLANES, t), _F32)],
        compiler_params=_params("parallel", "parallel"),
        name="mixer_natten_out",
    )(x, tok, proj3, k, v, w_out, g_ffn, w_router2)


def _pack_halves(h):
    half = D_MODEL // 2
    hi = pltpu.bitcast(h[:, :half].astype(_BF16).astype(_F32), jnp.uint32)
    lo = pltpu.bitcast(h[:, half:].astype(_BF16).astype(_F32), jnp.uint32)
    return hi | (lo >> 16)


def _unpack_halves(w):
    hi = pltpu.bitcast(w & jnp.uint32(0xFFFF0000), _F32).astype(_BF16)
    lo = pltpu.bitcast(w << 16, _F32).astype(_BF16)
    return jnp.concatenate([hi, lo], axis=1)


def _route(h, wr_ref):
    h_head = h.astype(_BF16)
    h_rest = (h - h_head.astype(_F32)).astype(_BF16)
    both = _dot(h_head, wr_ref[...])
    logits = both[:, :LANES] + both[:, LANES:] + _dot(h_rest, wr_ref[:, :LANES])
    assert N_EXPERTS == SUBLANES
    lt = jnp.concatenate([jnp.transpose(logits[c * LANES:(c + 1) * LANES])[:SUBLANES]
                          for c in range(h.shape[0] // LANES)], axis=1)
    expert = lax.broadcasted_iota(jnp.int32, lt.shape, 0)
    v1 = jnp.max(lt, axis=0, keepdims=True)
    i1 = jnp.min(jnp.where(lt == v1, expert, N_EXPERTS), axis=0, keepdims=True)
    lt2 = jnp.where(expert == i1, -jnp.inf, lt)
    v2 = jnp.max(lt2, axis=0, keepdims=True)
    i2 = jnp.min(jnp.where(lt2 == v2, expert, N_EXPERTS), axis=0, keepdims=True)
    e2 = jnp.exp(v2 - v1)
    g1 = 1.0 / (1.0 + e2)
    g2 = e2 / (1.0 + e2)
    idx = jnp.where(expert == 0, i1, jnp.where(expert == 1, i2, 0))
    gate = jnp.where(expert == 0, g1, jnp.where(expert == 1, g2, 0.0))
    return idx, gate


def _gather_rows(table, idx):
    n, d = idx.shape[0], table.shape[1]
    workers = SC_CORES * SC_SUBCORES
    per_worker = n // workers
    pairs = per_worker // (2 * SC_WINDOW)
    assert n == workers * pairs * 2 * SC_WINDOW, (n, workers, SC_WINDOW)
    mesh = plsc.VectorSubcoreMesh(core_axis_name="core", subcore_axis_name="subcore",
                                  num_cores=SC_CORES, num_subcores=SC_SUBCORES)

    def gather_kernel(table_hbm, idx_hbm, out_hbm, idx_a, idx_b, rows_a, rows_b, sem_a, sem_b):
        worker = lax.axis_index("subcore") * SC_CORES + lax.axis_index("core")
        base = worker * per_worker

        @pl.loop(0, pairs)
        def _(p):
            r0 = base + p * (2 * SC_WINDOW)
            r1 = r0 + SC_WINDOW
            pltpu.sync_copy(idx_hbm.at[pl.ds(r0, SC_WINDOW)], idx_a)
            gather_a = pltpu.async_copy(table_hbm.at[idx_a], rows_a, sem_a)
            pltpu.sync_copy(idx_hbm.at[pl.ds(r1, SC_WINDOW)], idx_b)
            gather_b = pltpu.async_copy(table_hbm.at[idx_b], rows_b, sem_b)
            gather_a.wait()
            pltpu.sync_copy(rows_a, out_hbm.at[pl.ds(r0, SC_WINDOW)])
            gather_b.wait()
            pltpu.sync_copy(rows_b, out_hbm.at[pl.ds(r1, SC_WINDOW)])

    idx_buf = pltpu.VMEM((SC_WINDOW,), jnp.int32)
    row_buf = pltpu.VMEM((SC_WINDOW, d), table.dtype)
    return pl.kernel(gather_kernel, out_type=jax.ShapeDtypeStruct((n, d), table.dtype), mesh=mesh,
                     scratch_types=[idx_buf, idx_buf, row_buf, row_buf,
                                    pltpu.SemaphoreType.DMA, pltpu.SemaphoreType.DMA],
                     name="moe_row_gather")(table, idx)


def _expert_kernel(meta_ref, xs_ref, wgu_ref, wd_ref, ys_ref, *, n_blocks):
    @pl.when(pl.program_id(0) < meta_ref[n_blocks])
    def _():
        h = _unpack_halves(xs_ref[...])
        acc = jnp.zeros((MOE_ROWS, D_MODEL), _F32)
        for c in range(D_FF_EXPERT // FF_CHUNK_EXPERT):
            lo = c * FF_CHUNK_EXPERT
            gate = _dot(h, wgu_ref[0, :, lo:lo + FF_CHUNK_EXPERT])
            up = _dot(h, wgu_ref[0, :, D_FF_EXPERT + lo:D_FF_EXPERT + lo + FF_CHUNK_EXPERT])
            a = (_silu(gate) * up).astype(_BF16)
            acc = acc + _dot(a, wd_ref[0, lo:lo + FF_CHUNK_EXPERT, :])
        ys_ref[...] = _pack_halves(acc)


def _experts(meta, xs, w_gu, w_down):
    n_rows, half = xs.shape
    n_blocks = n_rows // MOE_ROWS
    grid_spec = pltpu.PrefetchScalarGridSpec(
        num_scalar_prefetch=1,
        grid=(n_blocks,),
        in_specs=[pl.BlockSpec((MOE_ROWS, half), lambda i, m: (i, 0)),
                  pl.BlockSpec((1, D_MODEL, 2 * D_FF_EXPERT), lambda i, m: (m[i], 0, 0),
                               pipeline_mode=pl.Buffered(1)),
                  pl.BlockSpec((1, D_FF_EXPERT, D_MODEL), lambda i, m: (m[i], 0, 0),
                               pipeline_mode=pl.Buffered(1))],
        out_specs=pl.BlockSpec((MOE_ROWS, half), lambda i, m: (i, 0)),
    )
    return pl.pallas_call(
        functools.partial(_expert_kernel, n_blocks=n_blocks),
        grid_spec=grid_spec,
        out_shape=jax.ShapeDtypeStruct((n_rows, half), jnp.uint32),
        compiler_params=_params("arbitrary"),
        name="moe_experts",
    )(meta, xs, w_gu, w_down)


def _combine_kernel(x_ref, y0_ref, y1_ref, gate_ref, g_ref, o_ref, *, tc):
    gates = gate_ref[...]
    pad = jnp.zeros((LANES - SUBLANES, LANES), _F32)
    cols = [jnp.transpose(jnp.concatenate([gates[:, c * LANES:(c + 1) * LANES], pad], axis=0))
            for c in range(tc // LANES)]
    gcol = jnp.concatenate(cols, axis=0)
    y0 = _unpack_halves(y0_ref[...]).astype(_F32)
    y1 = _unpack_halves(y1_ref[...]).astype(_F32)
    f = y0 * gcol[:, 0:1] + y1 * gcol[:, 1:2]
    o_ref[...] = _rms(x_ref[...] + f, g_ref[...])


def _combine(x2, y0, y1, gates, g_final, *, tc):
    t = x2.shape[0]
    half = y0.shape[1]
    return pl.pallas_call(
        functools.partial(_combine_kernel, tc=tc),
        grid=(t // tc,),
        in_specs=[pl.BlockSpec((tc, D_MODEL), lambda i: (i, 0)),
                  pl.BlockSpec((tc, half), lambda i: (i, 0)),
                  pl.BlockSpec((tc, half), lambda i: (i, 0)),
                  pl.BlockSpec((SUBLANES, tc), lambda i: (0, i)),
                  _const_spec((1, D_MODEL))],
        out_specs=pl.BlockSpec((tc, D_MODEL), lambda i: (i, 0)),
        out_shape=jax.ShapeDtypeStruct(x2.shape, _F32),
        compiler_params=_params("parallel"),
        name="moe_combine",
    )(x2, y0, y1, gates, g_final)


def _routing_offsets(expert_idx, n_blocks):
    t = expert_idx.shape[1]
    n = 2 * t
    flat_e = expert_idx[:2].reshape(n)
    experts = jnp.arange(N_EXPERTS, dtype=jnp.int32)
    onehot = (flat_e[:, None] == experts[None, :]).astype(jnp.int32)
    running = jnp.cumsum(onehot, axis=0)
    rank = jnp.sum((running - onehot) * onehot, axis=1)
    counts = running[-1]
    padded = (counts + MOE_ROWS - 1) // MOE_ROWS * MOE_ROWS
    pad_end = jnp.cumsum(padded)
    pad_start = pad_end - padded
    start = jnp.cumsum(counts) - counts
    dest = (jnp.sum(onehot * pad_start[None, :], axis=1) + rank).astype(jnp.int32)

    block_row = jnp.arange(n_blocks, dtype=jnp.int32) * MOE_ROWS
    block_expert = jnp.minimum(jnp.sum((block_row[:, None] >= pad_end[None, :]).astype(jnp.int32), axis=1),
                               N_EXPERTS - 1)
    meta = jnp.concatenate([block_expert, pad_end[-1:] // MOE_ROWS]).astype(jnp.int32)

    assert N_EXPERTS * n < 2 ** 31
    order = jnp.sort(flat_e * n + jnp.arange(n, dtype=jnp.int32)) % n
    row_onehot = jnp.repeat((block_expert[:, None] == experts[None, :]).astype(jnp.int32), MOE_ROWS, axis=0)
    off = jnp.arange(n_blocks * MOE_ROWS, dtype=jnp.int32) - jnp.sum(row_onehot * pad_start[None, :], axis=1)
    valid = off < jnp.sum(row_onehot * counts[None, :], axis=1)
    pos = jnp.clip(jnp.sum(row_onehot * start[None, :], axis=1) + off, 0, n - 1)
    src_tok = jnp.where(valid, order[pos] % t, 0).astype(jnp.int32)
    return dest, src_tok, meta


def _dense_layers(x, mem, w, w_experts=None, run_after=None):
    b, s, d = x.shape
    t = b * s

    k0, v0, k1, v1 = _mem_kv(mem, w['norm_mem'], w['w_mem_kv'])
    x = _mixer_a(x, k0, v0, w['norm_mix'][0], w['w_in_a'], w['g_v_a'], w['w_s_a'], w['b_s_a'],
                 w['w_out'][0], tm=MIXER_TILE)
    side = () if w_experts is not None else (w['w_exp_gu'], w['w_exp_down'])
    x2, cast = _ffn(x.reshape(t, d), w['norm_ffn'][0], w['w_ffn_gu'], w['w_ffn_down'], side, tm=ROW_TILE)
    w_exp_gu, w_exp_down = w_experts if w_experts is not None else cast

    proj = _proj_b(x2, w['norm_mix'][1], w['w_in_b'], tm=MIXER_TILE).reshape(b, s, -1)
    tok = _natten(proj, _natten_bias_table(w['rpb_b'], s // GRID_W), run_after)
    x, h_packed, expert_idx, gates = _mixer_b_out(x2.reshape(b, s, d), tok, proj, k1, v1, w['w_out'][1],
                                                  w['norm_ffn'][1], w['w_router'], tm=MIXER_TILE)
    return x.reshape(t, d), h_packed, expert_idx, gates, (w_exp_gu, w_exp_down)


def _dispatch(h_packed, expert_idx):
    t = h_packed.shape[0]
    n_blocks = 2 * t // MOE_ROWS + N_EXPERTS
    dest, src_tok, meta = _routing_offsets(expert_idx, n_blocks)
    return _gather_rows(h_packed, src_tok), dest, src_tok, meta


def _experts_combine(x2, gates, xs, dest, meta, w_experts, g_final):
    t = x2.shape[0]
    ys = _experts(meta, xs, *w_experts)
    y0 = _gather_rows(ys, dest[:t])
    y1 = _gather_rows(ys, dest[t:])
    return _combine(x2, y0, y1, gates, g_final, tc=MIXER_TILE)


def _cast_kernel(w_ref, o_ref):
    o_ref[...] = w_ref[...].astype(o_ref.dtype)


def _to_bf16(w, rows_per_step):
    n, r, c = w.shape
    assert r % rows_per_step == 0
    block = (1, rows_per_step, c)
    return pl.pallas_call(
        _cast_kernel,
        grid=(n, r // rows_per_step),
        in_specs=[pl.BlockSpec(block, lambda i, j: (i, j, 0))],
        out_specs=pl.BlockSpec(block, lambda i, j: (i, j, 0)),
        out_shape=jax.ShapeDtypeStruct(w.shape, _BF16),
        compiler_params=_params("parallel", "parallel"),
        name="weight_cast",
    )(w)


def _split_router_weight(w):
    head = w.astype(_BF16)
    rest = (w - head.astype(_F32)).astype(_BF16)
    pad = ((0, 0), (0, LANES - N_EXPERTS))
    return jnp.concatenate([jnp.pad(head, pad), jnp.pad(rest, pad)], axis=1)


def _prepare_weights(norm_mix, norm_mem, norm_ffn, norm_final, w_in_a, g_v_a, w_s_a, b_s_a, w_in_b, rpb_b,
                     w_mem_kv, w_out, w_ffn_gu, w_ffn_down, w_router, w_exp_gu, w_exp_down):
    assert w_in_a.shape[0] == 1 and w_in_b.shape[0] == 1 and norm_mix.shape[0] == 2
    row = lambda a: a.astype(_F32)[:, None, :]
    return {
        'norm_mix': row(norm_mix), 'norm_mem': row(norm_mem), 'norm_ffn': row(norm_ffn),
        'norm_final': norm_final.astype(_F32)[None, :],
        'w_in_a': w_in_a[0].astype(_BF16),
        'g_v_a': g_v_a.astype(_F32),
        'w_s_a': jnp.concatenate([w_s_a[0, 0::2], w_s_a[0, 1::2]], axis=2).astype(_BF16),
        'b_s_a': jnp.repeat(b_s_a[0].astype(_F32).T, HEAD_DIM, axis=1),
        'w_in_b': w_in_b[0].astype(_BF16),
        'rpb_b': rpb_b[0],
        'w_mem_kv': w_mem_kv.astype(_BF16),
        'w_out': w_out.astype(_BF16),
        'w_ffn_gu': _to_bf16(w_ffn_gu, 256)[0],
        'w_ffn_down': _to_bf16(w_ffn_down, 704)[0],
        'w_router': _split_router_weight(w_router[0].astype(_F32)),
        'w_exp_gu': w_exp_gu[0].astype(_F32),
        'w_exp_down': w_exp_down[0].astype(_F32),
    }


def kernel(x_prompt, x_sample, mem_prompt, mem_sample, norm_mix, norm_mem, norm_ffn, norm_final, w_in_a, g_v_a, w_s_a, b_s_a, w_in_b, rpb_b, w_mem_kv, w_out, w_ffn_gu, w_ffn_down, w_router, w_exp_gu, w_exp_down):
    w = _prepare_weights(norm_mix, norm_mem, norm_ffn, norm_final, w_in_a, g_v_a, w_s_a, b_s_a, w_in_b, rpb_b,
                         w_mem_kv, w_out, w_ffn_gu, w_ffn_down, w_router, w_exp_gu, w_exp_down)
    x2_p, h_p, idx_p, gates_p, w_experts = _dense_layers(x_prompt, mem_prompt, w)
    xs_p, dest_p, _, meta_p = _dispatch(h_p, idx_p)
    x2_s, h_s, idx_s, gates_s, _ = _dense_layers(x_sample, mem_sample, w, w_experts, run_after=xs_p)
    xs_s, dest_s, _, meta_s = _dispatch(h_s, idx_s)
    y_prompt = _experts_combine(x2_p, gates_p, xs_p, dest_p, meta_p, w_experts, w['norm_final'])
    y_sample = _experts_combine(x2_s, gates_s, xs_s, dest_s, meta_s, w_experts, w['norm_final'])
    return (y_prompt.reshape(x_prompt.shape), y_sample.reshape(x_sample.shape))
```
